```python
import math
import jax, jax.numpy as jnp
from jax import lax
import numpy as np

D_MODEL = 1024
BATCH = 8
SEQ = 4096
DEPTH = 1

PLE_DIM = 256
D_FF = 2816
FFN_RES_WEIGHT = 0.5
NORM_EPS = 1e-6
Q_BLOCK = 128

MLA_HEADS = 8
MLA_NOPE = 64
MLA_ROPE = 32
MLA_QK = MLA_NOPE + MLA_ROPE
MLA_V = 64
Q_LORA = 384
KV_LORA = 256
ROPE_BASE = 10000.0

SB_HEADS = 8
SB_HEAD_DIM = 64
SB_WIDTH = SB_HEADS * SB_HEAD_DIM
MLA_WIDTH = MLA_HEADS * MLA_V

COL_CQ = Q_LORA
COL_CKV = KV_LORA
COL_KROPE = MLA_ROPE
COL_SB = 3 * SB_WIDTH
COL_GATES = 2 * D_MODEL
IN_COLS = COL_CQ + COL_CKV + COL_KROPE + COL_SB + COL_GATES
SPLITS = list(np.cumsum([COL_CQ, COL_CKV, COL_KROPE, COL_SB])[:])

kernel_name = "hybrid_mla_stickbreaking_macaron_ple"


def rms_norm(x, g):
    xf = x.astype(jnp.float32)
    r = lax.rsqrt(jnp.mean(xf * xf, axis=-1, keepdims=True) + NORM_EPS)
    return (xf * r).astype(x.dtype) * g


def apply_rope(x, positions):
    r = x.shape[-1]
    inv_freq = ROPE_BASE ** (-jnp.arange(0, r, 2, dtype=jnp.float32) / r)
    ang = positions.astype(jnp.float32)[..., None] * inv_freq
    cos = jnp.cos(ang)[:, :, None, :].astype(x.dtype)
    sin = jnp.sin(ang)[:, :, None, :].astype(x.dtype)
    x1, x2 = x[..., : r // 2], x[..., r // 2:]
    return jnp.concatenate([x1 * cos - x2 * sin, x2 * cos + x1 * sin], axis=-1)


def swiglu(u, w_in, w_out):
    a, b = jnp.split(u @ w_in, 2, axis=-1)
    return (jax.nn.silu(a) * b) @ w_out


def causal_softmax_attention(q, k, v):
    s_len = q.shape[2]
    scale = 1.0 / math.sqrt(q.shape[-1])
    outs = []
    for i in range(s_len // Q_BLOCK):
        k_len = (i + 1) * Q_BLOCK
        qb = q[:, :, i * Q_BLOCK:(i + 1) * Q_BLOCK]
        kb, vb = k[:, :, :k_len], v[:, :, :k_len]
        sc = jnp.einsum('bhqd,bhkd->bhqk', qb, kb).astype(jnp.float32) * scale
        q_pos = i * Q_BLOCK + jnp.arange(Q_BLOCK)
        mask = jnp.arange(k_len)[None, :] <= q_pos[:, None]
        w = jax.nn.softmax(jnp.where(mask, sc, -jnp.inf), axis=-1)
        outs.append(jnp.einsum('bhqk,bhkd->bhqd', w.astype(vb.dtype), vb))
    return jnp.concatenate(outs, axis=2)


def stick_breaking_attention(q, k, v):
    s_len = q.shape[2]
    scale = 1.0 / math.sqrt(q.shape[-1])
    outs = []
    for i in range(s_len // Q_BLOCK):
        k_len = (i + 1) * Q_BLOCK
        qb = q[:, :, i * Q_BLOCK:(i + 1) * Q_BLOCK]
        kb, vb = k[:, :, :k_len], v[:, :, :k_len]
        z = jnp.einsum('bhqd,bhkd->bhqk', qb, kb).astype(jnp.float32) * scale
        q_pos = i * Q_BLOCK + jnp.arange(Q_BLOCK)
        mask = jnp.arange(k_len)[None, :] < q_pos[:, None]
        log_1m = jnp.where(mask, jax.nn.log_sigmoid(-z), 0.0)
        suffix = lax.cumsum(log_1m, axis=3, reverse=True) - log_1m
        a = jnp.where(mask, jnp.exp(jax.nn.log_sigmoid(z) + suffix), 0.0)
        outs.append(jnp.einsum('bhqk,bhkd->bhqd', a.astype(vb.dtype), vb))
    return jnp.concatenate(outs, axis=2)


def setup_inputs(seed: int = 0) -> dict:
    key = jax.random.key(seed)
    ks = iter(jax.random.split(key, 32))
    f32 = jnp.float32

    def w(shape, fan_in):
        return jax.random.normal(next(ks), (DEPTH,) + shape, f32) * fan_in ** -0.5

    def gain(n):
        return 1.0 + 0.02 * jax.random.normal(next(ks), (DEPTH, n), f32)

    x = jax.random.normal(next(ks), (BATCH, SEQ, D_MODEL), f32)
    p = jax.random.normal(next(ks), (DEPTH, BATCH, SEQ, PLE_DIM), f32)
    offset = jax.random.randint(next(ks), (BATCH, 1), 0, 1024, dtype=jnp.int32)
    positions = offset + jnp.arange(SEQ, dtype=jnp.int32)[None, :]
    return {
        "x": x,
        "p": p,
        "positions": positions,
        "ffn1_norm": gain(D_MODEL),
        "ffn1_w_in": w((D_MODEL, 2 * D_FF), D_MODEL),
        "ffn1_w_out": w((D_FF, D_MODEL), D_FF),
        "mix_norm": gain(D_MODEL),
        "w_in": w((D_MODEL, IN_COLS), D_MODEL),
        "q_latent_norm": gain(Q_LORA),
        "w_q_up": w((Q_LORA, MLA_HEADS * MLA_QK), Q_LORA),
        "kv_latent_norm": gain(KV_LORA),
        "w_kv_up": w((KV_LORA, MLA_HEADS * (MLA_NOPE + MLA_V)), KV_LORA),
        "q_head_norm": gain(MLA_QK),
        "k_head_norm": gain(MLA_QK),
        "w_branch_mla": w((MLA_WIDTH, D_MODEL), MLA_WIDTH),
        "w_branch_sb": w((SB_WIDTH, D_MODEL), SB_WIDTH),
        "w_out": w((D_MODEL, D_MODEL), D_MODEL),
        "ffn2_norm": gain(D_MODEL),
        "ffn2_w_in": w((D_MODEL, 2 * D_FF), D_MODEL),
        "ffn2_w_out": w((D_FF, D_MODEL), D_FF),
        "ple_norm": gain(D_MODEL),
        "w_ple_gate": w((D_MODEL, D_MODEL), D_MODEL),
        "w_ple_proj": w((PLE_DIM, D_MODEL), PLE_DIM),
    }


def reference(x, p, positions, ffn1_norm, ffn1_w_in, ffn1_w_out, mix_norm, w_in,
              q_latent_norm, w_q_up, kv_latent_norm, w_kv_up, q_head_norm, k_head_norm,
              w_branch_mla, w_branch_sb, w_out, ffn2_norm, ffn2_w_in, ffn2_w_out,
              ple_norm, w_ple_gate, w_ple_proj):
    b, s, _ = x.shape
    h = x
    for i in range(DEPTH):
        h = h + FFN_RES_WEIGHT * swiglu(rms_norm(h, ffn1_norm[i]), ffn1_w_in[i], ffn1_w_out[i])

        u = rms_norm(h, mix_norm[i])
        proj = u @ w_in[i]
        c_q, c_kv, k_rope, sb_qkv, gates = jnp.split(proj, SPLITS, axis=-1)

        q = (rms_norm(c_q, q_latent_norm[i]) @ w_q_up[i]).reshape(b, s, MLA_HEADS, MLA_QK)
        kv = (rms_norm(c_kv, kv_latent_norm[i]) @ w_kv_up[i]).reshape(b, s, MLA_HEADS, MLA_NOPE + MLA_V)
        k_nope, v_mla = kv[..., :MLA_NOPE], kv[..., MLA_NOPE:]
        k_r = jnp.broadcast_to(k_rope[:, :, None, :], (b, s, MLA_HEADS, MLA_ROPE))
        k = jnp.concatenate([k_nope, k_r], axis=-1)
        q = rms_norm(q, q_head_norm[i])
        k = rms_norm(k, k_head_norm[i])
        q = jnp.concatenate([q[..., :MLA_NOPE], apply_rope(q[..., MLA_NOPE:], positions)], axis=-1)
        k = jnp.concatenate([k[..., :MLA_NOPE], apply_rope(k[..., MLA_NOPE:], positions)], axis=-1)
        o_mla = causal_softmax_attention(q.transpose(0, 2, 1, 3), k.transpose(0, 2, 1, 3),
                                         v_mla.transpose(0, 2, 1, 3))
        o_mla = o_mla.transpose(0, 2, 1, 3).reshape(b, s, MLA_WIDTH)

        sq, sk, sv = [t.reshape(b, s, SB_HEADS, SB_HEAD_DIM).transpose(0, 2, 1, 3)
                      for t in jnp.split(sb_qkv, 3, axis=-1)]
        o_sb = stick_breaking_attention(sq, sk, sv)
        o_sb = o_sb.transpose(0, 2, 1, 3).reshape(b, s, SB_WIDTH)

        g_mla, g_sb = jnp.split(jax.nn.sigmoid(gates), 2, axis=-1)
        merged = g_mla * (o_mla @ w_branch_mla[i]) + g_sb * (o_sb @ w_branch_sb[i])
        h = h + merged @ w_out[i]

        h = h + FFN_RES_WEIGHT * swiglu(rms_norm(h, ffn2_norm[i]), ffn2_w_in[i], ffn2_w_out[i])

        ple_gate = jax.nn.sigmoid(rms_norm(h, ple_norm[i]) @ w_ple_gate[i])
        h = h + ple_gate * (p[i] @ w_ple_proj[i])
    return h
```

```python
import functools
import math

import jax
import jax.numpy as jnp
from jax import lax
from jax.experimental import pallas as pl
from jax.experimental.pallas import tpu as pltpu

D_MODEL = 1024
D_FF = 2816
FFN_RES_WEIGHT = 0.5
NORM_EPS = 1e-6

MLA_HEADS = 8
MLA_NOPE = 64
MLA_ROPE = 32
MLA_QK = MLA_NOPE + MLA_ROPE
MLA_V = 64
Q_LORA = 384
KV_LORA = 256
ROPE_BASE = 10000.0

SB_HEADS = 8
SB_HEAD_DIM = 64
SB_WIDTH = SB_HEADS * SB_HEAD_DIM
MLA_WIDTH = MLA_HEADS * MLA_V

LANES = 128
HEAD_PAD = LANES
MLA_QK_PAD = MLA_HEADS * HEAD_PAD
HALF_ROPE = MLA_ROPE // 2

VMEM_LIMIT_BYTES = 56 * 1024 * 1024
TOKEN_TILE = 512
ATTN_TILE = 256
FF_CHUNKS = ((0, 1024), (1024, 1024), (2048, 768))

BF16 = jnp.bfloat16
F32 = jnp.float32


def _dot(a, b):
    return jnp.dot(a, b, preferred_element_type=F32)


def _dot_nt(a, b):
    return lax.dot_general(a, b, (((1,), (1,)), ((), ())), preferred_element_type=F32)


def _rms(x, g, n=None):
    n = x.shape[-1] if n is None else n
    ms = jnp.sum(x * x, axis=-1, keepdims=True) * (1.0 / n)
    return x * lax.rsqrt(ms + NORM_EPS) * g


def _sigmoid(x):
    return 1.0 / (1.0 + jnp.exp(-x))


def _const_spec(shape):
    nd = len(shape)
    return pl.BlockSpec(shape, lambda *_: (0,) * nd, pipeline_mode=pl.Buffered(1))


def _row_spec(tile, width):
    return pl.BlockSpec((tile, width), lambda i: (i, 0))


def _params(n_axes):
    return pltpu.CompilerParams(dimension_semantics=("arbitrary",) * n_axes,
                                vmem_limit_bytes=VMEM_LIMIT_BYTES)


def _ffn_kernel(*refs, with_ple):
    if with_ple:
        h_ref, g_ref, win_ref, wout_ref, p_ref, pg_ref, wpg_ref, wpp_ref, o_ref = refs
    else:
        h_ref, g_ref, win_ref, wout_ref, o_ref = refs
    h = h_ref[...]
    u = _rms(h, g_ref[...]).astype(BF16)
    acc = None
    for c0, cw in FF_CHUNKS:
        a = _dot(u, win_ref[:, c0:c0 + cw])
        b = _dot(u, win_ref[:, D_FF + c0:D_FF + c0 + cw])
        hm = (a * _sigmoid(a) * b).astype(BF16)
        y = _dot(hm, wout_ref[c0:c0 + cw, :])
        acc = y if acc is None else acc + y
    h = h + FFN_RES_WEIGHT * acc
    if with_ple:
        gate = _sigmoid(_dot(_rms(h, pg_ref[...]).astype(BF16), wpg_ref[...]))
        h = h + gate * _dot(p_ref[...].astype(BF16), wpp_ref[...])
    o_ref[...] = h


def _ffn(h, g, w_in, w_out, ple=None):
    t = h.shape[0]
    tm = min(TOKEN_TILE, t)
    ins = [h, g, w_in, w_out]
    specs = [_row_spec(tm, D_MODEL), _const_spec(g.shape), _const_spec(w_in.shape), _const_spec(w_out.shape)]
    if ple is not None:
        p, pg, wpg, wpp = ple
        ins += [p, pg, wpg, wpp]
        specs += [_row_spec(tm, p.shape[1]), _const_spec(pg.shape), _const_spec(wpg.shape), _const_spec(wpp.shape)]
    return pl.pallas_call(
        functools.partial(_ffn_kernel, with_ple=ple is not None),
        out_shape=jax.ShapeDtypeStruct(h.shape, F32),
        grid=(t // tm,),
        in_specs=specs,
        out_specs=_row_spec(tm, D_MODEL),
        compiler_params=_params(1),
        name="ffn_ple" if ple is not None else "ffn",
    )(*ins)


def _rope_partner(x):
    lane = lax.broadcasted_iota(jnp.int32, x.shape, 1)
    return jnp.where(lane < MLA_NOPE + HALF_ROPE,
                     pltpu.roll(x, HEAD_PAD - HALF_ROPE, 1), pltpu.roll(x, HALF_ROPE, 1))


def _head_norm_rope(x, g, cosf, sinf):
    x = _rms(x, g, MLA_QK)
    return x * cosf + _rope_partner(x) * sinf


def _mix_proj_kernel(h_ref, gmix_ref, wlat_ref, wsb_ref, gq_ref, wq_ref, gkv_ref, wkv_ref, gqh_ref, gkh_ref,
                     cos_ref, sin_ref, q_ref, k_ref, v_ref, sq_ref, sk_ref, sv_ref):
    u = _rms(h_ref[...], gmix_ref[...]).astype(BF16)
    lat = _dot(u, wlat_ref[...])
    sb = _dot(u, wsb_ref[...])
    scale_sb = 1.0 / math.sqrt(SB_HEAD_DIM)
    sq_ref[...] = (sb[:, :SB_WIDTH] * scale_sb).astype(BF16)
    sk_ref[...] = sb[:, SB_WIDTH:2 * SB_WIDTH].astype(BF16)
    sv_ref[...] = sb[:, 2 * SB_WIDTH:].astype(BF16)

    c_q = lat[:, :Q_LORA]
    c_kv = lat[:, Q_LORA:Q_LORA + KV_LORA]
    k_rope = lat[:, Q_LORA + KV_LORA:]
    q = _dot(_rms(c_q, gq_ref[...]).astype(BF16), wq_ref[...])
    kv = _dot(_rms(c_kv, gkv_ref[...]).astype(BF16), wkv_ref[...])
    v_ref[...] = kv[:, MLA_QK_PAD:].astype(BF16)
    cosf, sinf = cos_ref[...], sin_ref[...]
    gqh, gkh = gqh_ref[...], gkh_ref[...]
    for hd in range(MLA_HEADS):
        sl = slice(hd * HEAD_PAD, (hd + 1) * HEAD_PAD)
        q_ref[:, sl] = _head_norm_rope(q[:, sl], gqh, cosf, sinf).astype(BF16)
        k_ref[:, sl] = _head_norm_rope(kv[:, sl] + k_rope, gkh, cosf, sinf).astype(BF16)


def _mix_proj(h, gmix, wlat, wsb, gq, wq, gkv, wkv, gqh, gkh, cosf, sinf):
    t = h.shape[0]
    tm = min(TOKEN_TILE, t)
    consts = [gmix, wlat, wsb, gq, wq, gkv, wkv, gqh, gkh]
    out_widths = [MLA_QK_PAD, MLA_QK_PAD, MLA_WIDTH, SB_WIDTH, SB_WIDTH, SB_WIDTH]
    return pl.pallas_call(
        _mix_proj_kernel,
        out_shape=[jax.ShapeDtypeStruct((t, w), BF16) for w in out_widths],
        grid=(t // tm,),
        in_specs=[_row_spec(tm, D_MODEL)] + [_const_spec(c.shape) for c in consts]
                 + [_row_spec(tm, HEAD_PAD), _row_spec(tm, HEAD_PAD)],
        out_specs=[_row_spec(tm, w) for w in out_widths],
        compiler_params=_params(1),
        name="mix_proj",
    )(h, *consts, cosf, sinf)


def _attn_specs(tq, s, qk_width):
    q_spec = pl.BlockSpec((1, tq, qk_width), lambda b, p, i: (b, i, p))
    k_spec = pl.BlockSpec((1, s, qk_width), lambda b, p, i: (b, 0, p))
    v_spec = pl.BlockSpec((1, s, LANES), lambda b, p, i: (b, 0, p))
    o_spec = pl.BlockSpec((1, tq, LANES), lambda b, p, i: (b, i, p))
    return [q_spec, k_spec, v_spec], o_spec


def _low_head_lanes(shape):
    return lax.broadcasted_iota(jnp.int32, shape, 1) < LANES // 2


def _mla_attn_kernel(q_ref, k_ref, v_ref, o_ref, *, tile):
    i = pl.program_id(2)
    scale = 1.0 / math.sqrt(MLA_QK)
    row = lax.broadcasted_iota(jnp.int32, (tile, tile), 0)
    col = lax.broadcasted_iota(jnp.int32, (tile, tile), 1)
    causal = col <= row
    outs = []
    for hd in range(2):
        sl = slice(hd * HEAD_PAD, (hd + 1) * HEAD_PAD)
        q = q_ref[0, :, sl]

        def block(j, carry, masked):
            m, l, acc = carry
            rows = pl.ds(pl.multiple_of(j * tile, tile), tile)
            s = _dot_nt(q, k_ref[0, rows, sl]) * scale
            if masked:
                s = jnp.where(causal, s, -jnp.inf)
            m_new = jnp.maximum(m, jnp.max(s, axis=-1, keepdims=True))
            alpha = jnp.exp(m - m_new)
            p = jnp.exp(s - m_new)
            l = alpha * l + jnp.sum(p, axis=-1, keepdims=True)
            acc = alpha * acc + _dot(p.astype(BF16), v_ref[0, rows, :])
            return m_new, l, acc

        init = (jnp.full((tile, 1), -jnp.inf, F32), jnp.zeros((tile, 1), F32), jnp.zeros((tile, LANES), F32))
        carry = lax.fori_loop(0, i, functools.partial(block, masked=False), init)
        _, l, acc = block(i, carry, True)
        outs.append(acc / l)
    o_ref[0] = jnp.where(_low_head_lanes((tile, LANES)), outs[0], outs[1]).astype(BF16)


def _sb_attn_kernel(q_ref, k_ref, v_ref, o_ref, *, tile):
    i = pl.program_id(2)
    row = lax.broadcasted_iota(jnp.int32, (tile, tile), 0)
    col = lax.broadcasted_iota(jnp.int32, (tile, tile), 1)
    strict = col < row
    later = (row > col).astype(BF16)
    q_pair = q_ref[0]
    low = _low_head_lanes(q_pair.shape)
    outs = []
    for hd in range(2):
        q = jnp.where(low if hd == 0 else jnp.logical_not(low), q_pair, jnp.zeros_like(q_pair))

        def block(j, carry, masked):
            surv, acc = carry
            rows = pl.ds(pl.multiple_of(j * tile, tile), tile)
            z = _dot_nt(q, k_ref[0, rows, :])
            log_1m = -(jnp.maximum(z, 0.0) + jnp.log(1.0 + jnp.exp(-jnp.abs(z))))
            log_p = z + log_1m
            if masked:
                log_1m = jnp.where(strict, log_1m, 0.0)
            hi = log_1m.astype(BF16)
            lo = (log_1m - hi.astype(F32)).astype(BF16)
            suffix = _dot(hi, later) + _dot(lo, later) + surv
            a = jnp.exp(log_p + suffix)
            if masked:
                a = jnp.where(strict, a, 0.0)
            acc = acc + _dot(a.astype(BF16), v_ref[0, rows, :])
            return surv + jnp.sum(log_1m, axis=-1, keepdims=True), acc

        carry = block(i, (jnp.zeros((tile, 1), F32), jnp.zeros((tile, LANES), F32)), True)
        _, acc = lax.fori_loop(0, i, lambda t, c: block(i - 1 - t, c, False), carry)
        outs.append(acc)
    o_ref[0] = jnp.where(low, outs[0], outs[1]).astype(BF16)


def _attention(kernel_fn, name, q, k, v, qk_width):
    b, s, _ = v.shape
    tile = min(ATTN_TILE, s)
    in_specs, out_spec = _attn_specs(tile, s, qk_width)
    return pl.pallas_call(
        functools.partial(kernel_fn, tile=tile),
        out_shape=jax.ShapeDtypeStruct(v.shape, BF16),
        grid=(b, v.shape[2] // LANES, s // tile),
        in_specs=in_specs,
        out_specs=out_spec,
        compiler_params=_params(3),
        name=name,
    )(q, k, v)


def _merge_kernel(h_ref, gmix_ref, wg_ref, om_ref, os_ref, wbm_ref, wbs_ref, wo_ref, o_ref):
    h = h_ref[...]
    u = _rms(h, gmix_ref[...]).astype(BF16)
    gates = _sigmoid(_dot(u, wg_ref[...]))
    merged = gates[:, :D_MODEL] * _dot(om_ref[...], wbm_ref[...]) + gates[:, D_MODEL:] * _dot(os_ref[...], wbs_ref[...])
    o_ref[...] = h + _dot(merged.astype(BF16), wo_ref[...])


def _merge(h, gmix, wg, om, osb, wbm, wbs, wo):
    t = h.shape[0]
    tm = min(TOKEN_TILE, t)
    return pl.pallas_call(
        _merge_kernel,
        out_shape=jax.ShapeDtypeStruct(h.shape, F32),
        grid=(t // tm,),
        in_specs=[_row_spec(tm, D_MODEL), _const_spec(gmix.shape), _const_spec(wg.shape),
                  _row_spec(tm, MLA_WIDTH), _row_spec(tm, SB_WIDTH),
                  _const_spec(wbm.shape), _const_spec(wbs.shape), _const_spec(wo.shape)],
        out_specs=_row_spec(tm, D_MODEL),
        compiler_params=_params(1),
        name="merge",
    )(h, gmix, wg, om, osb, wbm, wbs, wo)


def _pad_heads(w, head_dim, real):
    lead = w.shape[:-1]
    w = w.reshape(lead + (-1, head_dim))[..., :real]
    w = jnp.pad(w, [(0, 0)] * len(lead) + [(0, 0), (0, HEAD_PAD - real)])
    return w.reshape(lead + (-1,))


def _rope_tables(positions):
    inv_freq = ROPE_BASE ** (-jnp.arange(0, MLA_ROPE, 2, dtype=F32) / MLA_ROPE)
    ang = positions.astype(F32).reshape(-1, 1) * inv_freq
    cos, sin = jnp.cos(ang), jnp.sin(ang)
    t = ang.shape[0]
    tail = HEAD_PAD - MLA_QK
    cosf = jnp.concatenate([jnp.ones((t, MLA_NOPE), F32), cos, cos, jnp.ones((t, tail), F32)], axis=-1)
    sinf = jnp.concatenate([jnp.zeros((t, MLA_NOPE), F32), -sin, sin, jnp.zeros((t, tail), F32)], axis=-1)
    return cosf, sinf


def _layer(h, p, cosf, sinf, b, s, ffn1_norm, ffn1_w_in, ffn1_w_out, mix_norm, w_in, q_latent_norm, w_q_up,
           kv_latent_norm, w_kv_up, q_head_norm, k_head_norm, w_branch_mla, w_branch_sb, w_out, ffn2_norm,
           ffn2_w_in, ffn2_w_out, ple_norm, w_ple_gate, w_ple_proj):
    row = lambda g: g.reshape(1, -1)
    c0, c1, c2, c3 = Q_LORA, Q_LORA + KV_LORA, Q_LORA + KV_LORA + MLA_ROPE, Q_LORA + KV_LORA + MLA_ROPE + 3 * SB_WIDTH
    w_krope = jnp.pad(w_in[:, c1:c2], ((0, 0), (MLA_NOPE, HEAD_PAD - MLA_QK)))
    w_lat = jnp.concatenate([w_in[:, :c1], w_krope], axis=1).astype(BF16)
    w_sb = w_in[:, c2:c3].astype(BF16)
    w_gates = w_in[:, c3:].astype(BF16)
    w_q = _pad_heads(w_q_up, MLA_QK, MLA_QK).astype(BF16)
    kv_heads = w_kv_up.reshape(KV_LORA, MLA_HEADS, MLA_NOPE + MLA_V)
    w_knope = _pad_heads(kv_heads[..., :MLA_NOPE].reshape(KV_LORA, -1), MLA_NOPE, MLA_NOPE)
    w_kv = jnp.concatenate([w_knope, kv_heads[..., MLA_NOPE:].reshape(KV_LORA, -1)], axis=1).astype(BF16)
    gqh = jnp.pad(q_head_norm, (0, HEAD_PAD - MLA_QK)).reshape(1, -1)
    gkh = jnp.pad(k_head_norm, (0, HEAD_PAD - MLA_QK)).reshape(1, -1)

    h = _ffn(h, row(ffn1_norm), ffn1_w_in.astype(BF16), ffn1_w_out.astype(BF16))
    q, k, v, sq, sk, sv = _mix_proj(h, row(mix_norm), w_lat, w_sb, row(q_latent_norm), w_q, row(kv_latent_norm),
                                    w_kv, gqh, gkh, cosf, sinf)
    seq = lambda a: a.reshape(b, s, a.shape[-1])
    o_mla = _attention(_mla_attn_kernel, "mla_attn", seq(q), seq(k), seq(v), 2 * HEAD_PAD)
    o_sb = _attention(_sb_attn_kernel, "sb_attn", seq(sq), seq(sk), seq(sv), LANES)
    h = _merge(h, row(mix_norm), w_gates, o_mla.reshape(b * s, -1), o_sb.reshape(b * s, -1),
               w_branch_mla.astype(BF16), w_branch_sb.astype(BF16), w_out.astype(BF16))
    return _ffn(h, row(ffn2_norm), ffn2_w_in.astype(BF16), ffn2_w_out.astype(BF16),
                ple=(p, row(ple_norm), w_ple_gate.astype(BF16), w_ple_proj.astype(BF16)))


def kernel(x, p, positions, ffn1_norm, ffn1_w_in, ffn1_w_out, mix_norm, w_in, q_latent_norm, w_q_up, kv_latent_norm, w_kv_up, q_head_norm, k_head_norm, w_branch_mla, w_branch_sb, w_out, ffn2_norm, ffn2_w_in, ffn2_w_out, ple_norm, w_ple_gate, w_ple_proj):
    b, s, _ = x.shape
    cosf, sinf = _rope_tables(positions)
    weights = (ffn1_norm, ffn1_w_in, ffn1_w_out, mix_norm, w_in, q_latent_norm, w_q_up, kv_latent_norm, w_kv_up,
               q_head_norm, k_head_norm, w_branch_mla, w_branch_sb, w_out, ffn2_norm, ffn2_w_in, ffn2_w_out,
               ple_norm, w_ple_gate, w_ple_proj)
    h = x.reshape(b * s, D_MODEL)
    for i in range(p.shape[0]):
        h = _layer(h, p[i].reshape(b * s, -1), cosf, sinf, b, s, *(w[i] for w in weights))
    return h.reshape(b, s, D_MODEL)
```

```python
import functools
import math

import jax
import jax.numpy as jnp
from jax import lax
from jax.experimental import pallas as pl
from jax.experimental.pallas import tpu as pltpu

D_MODEL = 1024
D_FF = 2816
FFN_RES_WEIGHT = 0.5
NORM_EPS = 1e-6

MLA_HEADS = 8
MLA_NOPE = 64
MLA_ROPE = 32
MLA_QK = MLA_NOPE + MLA_ROPE
MLA_V = 64
Q_LORA = 384
KV_LORA = 256
ROPE_BASE = 10000.0

SB_HEADS = 8
SB_HEAD_DIM = 64
SB_WIDTH = SB_HEADS * SB_HEAD_DIM
MLA_WIDTH = MLA_HEADS * MLA_V
V_HEAD = 64

LANES = 128
HEAD_PAD = LANES
MLA_QK_PAD = MLA_HEADS * HEAD_PAD
HALF_ROPE = MLA_ROPE // 2

VMEM_LIMIT_BYTES = 56 * 1024 * 1024
TOKEN_TILE = 512
ATTN_TILE = 256
FF_CHUNKS = ((0, 1024), (1024, 1024), (2048, 768))
SB_DEAD_LOG_WEIGHT = -104.0

BF16 = jnp.bfloat16
F32 = jnp.float32


def _dot(a, b):
    return jnp.dot(a, b, preferred_element_type=F32)


def _dot_nt(a, b):
    return lax.dot_general(a, b, (((1,), (1,)), ((), ())), preferred_element_type=F32)


def _rms(x, g, n=None):
    n = x.shape[-1] if n is None else n
    ms = jnp.sum(x * x, axis=-1, keepdims=True) * (1.0 / n)
    return x * lax.rsqrt(ms + NORM_EPS) * g


def _sigmoid(x):
    return 1.0 / (1.0 + jnp.exp(-x))


def _const_spec(shape):
    nd = len(shape)
    return pl.BlockSpec(shape, lambda *_: (0,) * nd, pipeline_mode=pl.Buffered(1))


def _row_spec(tile, width):
    return pl.BlockSpec((tile, width), lambda i: (i, 0))


def _params(n_axes):
    return pltpu.CompilerParams(dimension_semantics=("arbitrary",) * n_axes,
                                vmem_limit_bytes=VMEM_LIMIT_BYTES)


def _ffn_kernel(*refs, with_ple):
    if with_ple:
        h_ref, g_ref, win_ref, wout_ref, p_ref, pg_ref, wpg_ref, wpp_ref, o_ref = refs
    else:
        h_ref, g_ref, win_ref, wout_ref, o_ref = refs
    h = h_ref[...]
    u = _rms(h, g_ref[...]).astype(BF16)
    acc = None
    for c0, cw in FF_CHUNKS:
        a = _dot(u, win_ref[:, c0:c0 + cw])
        b = _dot(u, win_ref[:, D_FF + c0:D_FF + c0 + cw])
        hm = (a * _sigmoid(a) * b).astype(BF16)
        y = _dot(hm, wout_ref[c0:c0 + cw, :])
        acc = y if acc is None else acc + y
    h = h + FFN_RES_WEIGHT * acc
    if with_ple:
        gate = _sigmoid(_dot(_rms(h, pg_ref[...]).astype(BF16), wpg_ref[...]))
        h = h + gate * _dot(p_ref[...].astype(BF16), wpp_ref[...])
    o_ref[...] = h


def _ffn(h, g, w_in, w_out, ple=None):
    t = h.shape[0]
    tm = min(TOKEN_TILE, t)
    ins = [h, g, w_in, w_out]
    specs = [_row_spec(tm, D_MODEL), _const_spec(g.shape), _const_spec(w_in.shape), _const_spec(w_out.shape)]
    if ple is not None:
        p, pg, wpg, wpp = ple
        ins += [p, pg, wpg, wpp]
        specs += [_row_spec(tm, p.shape[1]), _const_spec(pg.shape), _const_spec(wpg.shape), _const_spec(wpp.shape)]
    return pl.pallas_call(
        functools.partial(_ffn_kernel, with_ple=ple is not None),
        out_shape=jax.ShapeDtypeStruct(h.shape, F32),
        grid=(t // tm,),
        in_specs=specs,
        out_specs=_row_spec(tm, D_MODEL),
        compiler_params=_params(1),
        name="ffn_ple" if ple is not None else "ffn",
    )(*ins)


def _rope_partner(x):
    lane = lax.broadcasted_iota(jnp.int32, x.shape, 1)
    return jnp.where(lane < MLA_NOPE + HALF_ROPE,
                     pltpu.roll(x, HEAD_PAD - HALF_ROPE, 1), pltpu.roll(x, HALF_ROPE, 1))


def _head_norm_rope(x, g, cosf, sinf):
    x = _rms(x, g, MLA_QK)
    return x * cosf + _rope_partner(x) * sinf


def _store_key_blocks_transposed(vt_ref, v, tile):
    for c in range(v.shape[0] // tile):
        vt_ref[c] = v[c * tile:(c + 1) * tile, :].T.astype(BF16)


def _mix_proj_kernel(h_ref, gmix_ref, wlat_ref, wsb_ref, gq_ref, wq_ref, gkv_ref, wkv_ref, gqh_ref, gkh_ref,
                     cos_ref, sin_ref, q_ref, k_ref, vt_ref, sq_ref, sk_ref, svt_ref, *, tile):
    u = _rms(h_ref[...], gmix_ref[...]).astype(BF16)
    lat = _dot(u, wlat_ref[...])
    sb = _dot(u, wsb_ref[...])
    scale_sb = 1.0 / math.sqrt(SB_HEAD_DIM)
    sq_ref[...] = (sb[:, :SB_WIDTH] * scale_sb).astype(BF16)
    sk_ref[...] = sb[:, SB_WIDTH:2 * SB_WIDTH].astype(BF16)
    _store_key_blocks_transposed(svt_ref, sb[:, 2 * SB_WIDTH:], tile)

    c_q = lat[:, :Q_LORA]
    c_kv = lat[:, Q_LORA:Q_LORA + KV_LORA]
    k_rope = lat[:, Q_LORA + KV_LORA:]
    q = _dot(_rms(c_q, gq_ref[...]).astype(BF16), wq_ref[...])
    kv = _dot(_rms(c_kv, gkv_ref[...]).astype(BF16), wkv_ref[...])
    _store_key_blocks_transposed(vt_ref, kv[:, MLA_QK_PAD:], tile)
    cosf, sinf = cos_ref[...], sin_ref[...]
    gqh, gkh = gqh_ref[...], gkh_ref[...]
    for hd in range(MLA_HEADS):
        sl = slice(hd * HEAD_PAD, (hd + 1) * HEAD_PAD)
        q_ref[:, sl] = _head_norm_rope(q[:, sl], gqh, cosf, sinf).astype(BF16)
        k_ref[:, sl] = _head_norm_rope(kv[:, sl] + k_rope, gkh, cosf, sinf).astype(BF16)


def _mix_proj(h, gmix, wlat, wsb, gq, wq, gkv, wkv, gqh, gkh, cosf, sinf, tile):
    t = h.shape[0]
    tm = min(TOKEN_TILE, t)
    consts = [gmix, wlat, wsb, gq, wq, gkv, wkv, gqh, gkh]
    rows = lambda w: (jax.ShapeDtypeStruct((t, w), BF16), _row_spec(tm, w))
    key_blocks_t = lambda w: (jax.ShapeDtypeStruct((t // tile, w, tile), BF16),
                              pl.BlockSpec((tm // tile, w, tile), lambda i: (i, 0, 0)))
    outs = [rows(MLA_QK_PAD), rows(MLA_QK_PAD), key_blocks_t(MLA_WIDTH),
            rows(SB_WIDTH), rows(SB_WIDTH), key_blocks_t(SB_WIDTH)]
    return pl.pallas_call(
        functools.partial(_mix_proj_kernel, tile=tile),
        out_shape=[o[0] for o in outs],
        grid=(t // tm,),
        in_specs=[_row_spec(tm, D_MODEL)] + [_const_spec(c.shape) for c in consts]
                 + [_row_spec(tm, HEAD_PAD), _row_spec(tm, HEAD_PAD)],
        out_specs=[o[1] for o in outs],
        compiler_params=_params(1),
        name="mix_proj",
    )(h, *consts, cosf, sinf)


def _tile_iotas(tile):
    key = lax.broadcasted_iota(jnp.int32, (tile, tile), 0)
    qry = lax.broadcasted_iota(jnp.int32, (tile, tile), 1)
    return key, qry


def _key_rows(j, tile):
    return pl.ds(pl.multiple_of(j * tile, tile), tile)


def _mla_attn_kernel(q_ref, k_ref, vt_ref, o_ref, s_scr, p_scr, *, tile):
    i = pl.program_id(2)
    c = math.log2(math.e) / math.sqrt(MLA_QK)
    key, qry = _tile_iotas(tile)
    causal = key <= qry
    heads = [slice(hd * HEAD_PAD, (hd + 1) * HEAD_PAD) for hd in range(2)]
    qs = [q_ref[0, :, sl] for sl in heads]

    def scores(j, slot):
        rows = _key_rows(j, tile)
        for hd in range(2):
            s_scr[slot, hd] = _dot_nt(k_ref[0, rows, heads[hd]], qs[hd])

    def weighted_values(j, slot, alphas, accs):
        return tuple(alphas[hd] * accs[hd] + _dot(vt_ref[0, j, hd * V_HEAD:(hd + 1) * V_HEAD, :], p_scr[slot, hd])
                     for hd in range(2))

    def softmax(slot, ms, ls, masked):
        out = []
        for hd in range(2):
            s = s_scr[slot, hd]
            if masked:
                s = jnp.where(causal, s, -jnp.inf)
            m_new = jnp.maximum(ms[hd], jnp.max(s, axis=0, keepdims=True))
            alpha = jnp.exp2((ms[hd] - m_new) * c)
            p = jnp.exp2((s - m_new) * c)
            p_scr[slot, hd] = p.astype(BF16)
            out.append((m_new, alpha * ls[hd] + jnp.sum(p, axis=0, keepdims=True), alpha))
        return tuple(zip(*out))

    def step(j, slot, carry):
        ms, ls, alphas, accs = carry
        scores(j + 1, 1 - slot)
        accs = weighted_values(jnp.maximum(j - 1, 0), 1 - slot, alphas, accs)
        ms, ls, alphas = softmax(slot, ms, ls, False)
        return ms, ls, alphas, accs

    def finish(slot, carry):
        ms, ls, alphas, accs = carry
        accs = weighted_values(jnp.maximum(i - 1, 0), 1 - slot, alphas, accs)
        ms, ls, alphas = softmax(slot, ms, ls, True)
        accs = weighted_values(i, slot, alphas, accs)
        o_t = jnp.concatenate([accs[hd] / ls[hd] for hd in range(2)], axis=0)
        o_ref[0] = o_t.T.astype(BF16)

    row = lambda v: (jnp.full((1, tile), v, F32),) * 2
    p_scr[1] = jnp.zeros(p_scr.shape[1:], BF16)
    scores(0, 0)
    carry = (row(-jnp.inf), row(0.0), row(1.0), (jnp.zeros((V_HEAD, tile), F32),) * 2)
    carry = lax.fori_loop(0, i // 2, lambda t, cr: step(2 * t + 1, 1, step(2 * t, 0, cr)), carry)

    @pl.when(i % 2 == 0)
    def _():
        finish(0, carry)

    @pl.when(i % 2 == 1)
    def _():
        finish(1, step(i - 1, 0, carry))


def _sb_attn_kernel(q_ref, k_ref, vt_ref, o_ref, *, tile):
    i = pl.program_id(2)
    key, qry = _tile_iotas(tile)
    strict = key < qry
    from_key = (qry >= key).astype(BF16)
    q_pair = q_ref[0]
    low = lax.broadcasted_iota(jnp.int32, q_pair.shape, 1) < V_HEAD
    qs = [jnp.where(low, q_pair, jnp.zeros_like(q_pair)), jnp.where(low, jnp.zeros_like(q_pair), q_pair)]

    def two_blocks(j, survs, accs, masked):
        has_second = j >= 1
        idx = (j, jnp.maximum(j - 1, 0))
        k_blks = [k_ref[0, _key_rows(jb, tile), :] for jb in idx]
        zs = [[_dot_nt(k_blks[b], qs[hd]) for hd in range(2)] for b in range(2)]
        incl = [[None, None], [None, None]]
        for b in range(2):
            for hd in range(2):
                z = zs[b][hd]
                log_1m = -(jnp.maximum(z, 0.0) + jnp.log(1.0 + jnp.exp(-jnp.abs(z))))
                if masked and b == 0:
                    log_1m = jnp.where(strict, log_1m, 0.0)
                hi = log_1m.astype(BF16)
                lo = (log_1m - hi.astype(F32)).astype(BF16)
                incl[b][hd] = _dot(from_key, hi) + _dot(from_key, lo)
        new_survs, new_accs = [], []
        for hd in range(2):
            surv = [survs[hd], jnp.where(has_second, survs[hd] + incl[0][hd][0:1, :], -jnp.inf)]
            acc = accs[hd]
            for b in range(2):
                a = jnp.exp(zs[b][hd] + incl[b][hd] + surv[b])
                if masked and b == 0:
                    a = jnp.where(strict, a, 0.0)
                acc = acc + _dot(vt_ref[0, idx[b], hd * V_HEAD:(hd + 1) * V_HEAD, :], a.astype(BF16))
            new_survs.append(surv[1] + incl[1][hd][0:1, :])
            new_accs.append(acc)
        return tuple(new_survs), tuple(new_accs)

    def alive(survs):
        return (jnp.maximum(jnp.max(survs[0]), jnp.max(survs[1])) > SB_DEAD_LOG_WEIGHT).astype(jnp.int32)

    survs, accs = two_blocks(i, (jnp.zeros((1, tile), F32),) * 2, (jnp.zeros((V_HEAD, tile), F32),) * 2, True)

    def more(carry):
        j, live, _, _ = carry
        return jnp.logical_and(j >= 0, live > 0)

    def body(carry):
        j, _, survs, accs = carry
        survs, accs = two_blocks(j, survs, accs, False)
        return j - 2, alive(survs), survs, accs

    _, _, _, accs = lax.while_loop(more, body, (i - 2, alive(survs), survs, accs))
    o_t = jnp.concatenate(accs, axis=0)
    o_ref[0] = o_t.T.astype(BF16)


def _attention(kernel_fn, name, q, k, vt, qk_width, tile, scratch_dtypes=()):
    b, s, _ = q.shape
    n_pairs = vt.shape[2] // LANES
    return pl.pallas_call(
        functools.partial(kernel_fn, tile=tile),
        out_shape=jax.ShapeDtypeStruct((b, s, vt.shape[2]), BF16),
        grid=(b, n_pairs, s // tile),
        in_specs=[pl.BlockSpec((1, tile, qk_width), lambda b, p, i: (b, i, p)),
                  pl.BlockSpec((1, s, qk_width), lambda b, p, i: (b, 0, p)),
                  pl.BlockSpec((1, s // tile, LANES, tile), lambda b, p, i: (b, 0, p, 0))],
        out_specs=pl.BlockSpec((1, tile, LANES), lambda b, p, i: (b, i, p)),
        scratch_shapes=[pltpu.VMEM((2, 2, tile, tile), dt) for dt in scratch_dtypes],
        compiler_params=_params(3),
        name=name,
    )(q, k, vt)


def _merge_kernel(h_ref, gmix_ref, wg_ref, om_ref, os_ref, wbm_ref, wbs_ref, wo_ref, o_ref):
    h = h_ref[...]
    u = _rms(h, gmix_ref[...]).astype(BF16)
    gates = _sigmoid(_dot(u, wg_ref[...]))
    merged = gates[:, :D_MODEL] * _dot(om_ref[...], wbm_ref[...]) + gates[:, D_MODEL:] * _dot(os_ref[...], wbs_ref[...])
    o_ref[...] = h + _dot(merged.astype(BF16), wo_ref[...])


def _merge(h, gmix, wg, om, osb, wbm, wbs, wo):
    t = h.shape[0]
    tm = min(TOKEN_TILE, t)
    return pl.pallas_call(
        _merge_kernel,
        out_shape=jax.ShapeDtypeStruct(h.shape, F32),
        grid=(t // tm,),
        in_specs=[_row_spec(tm, D_MODEL), _const_spec(gmix.shape), _const_spec(wg.shape),
                  _row_spec(tm, MLA_WIDTH), _row_spec(tm, SB_WIDTH),
                  _const_spec(wbm.shape), _const_spec(wbs.shape), _const_spec(wo.shape)],
        out_specs=_row_spec(tm, D_MODEL),
        compiler_params=_params(1),
        name="merge",
    )(h, gmix, wg, om, osb, wbm, wbs, wo)


def _pad_heads(w, head_dim, real):
    lead = w.shape[:-1]
    w = w.reshape(lead + (-1, head_dim))[..., :real]
    w = jnp.pad(w, [(0, 0)] * len(lead) + [(0, 0), (0, HEAD_PAD - real)])
    return w.reshape(lead + (-1,))


def _rope_tables(positions):
    inv_freq = ROPE_BASE ** (-jnp.arange(0, MLA_ROPE, 2, dtype=F32) / MLA_ROPE)
    ang = positions.astype(F32).reshape(-1, 1) * inv_freq
    cos, sin = jnp.cos(ang), jnp.sin(ang)
    t = ang.shape[0]
    tail = HEAD_PAD - MLA_QK
    cosf = jnp.concatenate([jnp.ones((t, MLA_NOPE), F32), cos, cos, jnp.ones((t, tail), F32)], axis=-1)
    sinf = jnp.concatenate([jnp.zeros((t, MLA_NOPE), F32), -sin, sin, jnp.zeros((t, tail), F32)], axis=-1)
    return cosf, sinf


def _layer(h, p, cosf, sinf, b, s, ffn1_norm, ffn1_w_in, ffn1_w_out, mix_norm, w_in, q_latent_norm, w_q_up,
           kv_latent_norm, w_kv_up, q_head_norm, k_head_norm, w_branch_mla, w_branch_sb, w_out, ffn2_norm,
           ffn2_w_in, ffn2_w_out, ple_norm, w_ple_gate, w_ple_proj):
    row = lambda g: g.reshape(1, -1)
    c0, c1, c2, c3 = Q_LORA, Q_LORA + KV_LORA, Q_LORA + KV_LORA + MLA_ROPE, Q_LORA + KV_LORA + MLA_ROPE + 3 * SB_WIDTH
    w_krope = jnp.pad(w_in[:, c1:c2], ((0, 0), (MLA_NOPE, HEAD_PAD - MLA_QK)))
    w_lat = jnp.concatenate([w_in[:, :c1], w_krope], axis=1).astype(BF16)
    w_sb = w_in[:, c2:c3].astype(BF16)
    w_gates = w_in[:, c3:].astype(BF16)
    w_q = _pad_heads(w_q_up, MLA_QK, MLA_QK).astype(BF16)
    kv_heads = w_kv_up.reshape(KV_LORA, MLA_HEADS, MLA_NOPE + MLA_V)
    w_knope = _pad_heads(kv_heads[..., :MLA_NOPE].reshape(KV_LORA, -1), MLA_NOPE, MLA_NOPE)
    w_kv = jnp.concatenate([w_knope, kv_heads[..., MLA_NOPE:].reshape(KV_LORA, -1)], axis=1).astype(BF16)
    gqh = jnp.pad(q_head_norm, (0, HEAD_PAD - MLA_QK)).reshape(1, -1)
    gkh = jnp.pad(k_head_norm, (0, HEAD_PAD - MLA_QK)).reshape(1, -1)
    tile = min(ATTN_TILE, s)

    h = _ffn(h, row(ffn1_norm), ffn1_w_in.astype(BF16), ffn1_w_out.astype(BF16))
    q, k, vt, sq, sk, svt = _mix_proj(h, row(mix_norm), w_lat, w_sb, row(q_latent_norm), w_q, row(kv_latent_norm),
                                      w_kv, gqh, gkh, cosf, sinf, tile)
    seq = lambda a: a.reshape(b, s, a.shape[-1])
    key_blocks = lambda a: a.reshape(b, s // tile, a.shape[1], tile)
    o_mla = _attention(_mla_attn_kernel, "mla_attn", seq(q), seq(k), key_blocks(vt), 2 * HEAD_PAD, tile, (F32, BF16))
    o_sb = _attention(_sb_attn_kernel, "sb_attn", seq(sq), seq(sk), key_blocks(svt), LANES, tile)
    h = _merge(h, row(mix_norm), w_gates, o_mla.reshape(b * s, -1), o_sb.reshape(b * s, -1),
               w_branch_mla.astype(BF16), w_branch_sb.astype(BF16), w_out.astype(BF16))
    return _ffn(h, row(ffn2_norm), ffn2_w_in.astype(BF16), ffn2_w_out.astype(BF16),
                ple=(p, row(ple_norm), w_ple_gate.astype(BF16), w_ple_proj.astype(BF16)))


def kernel(x, p, positions, ffn1_norm, ffn1_w_in, ffn1_w_out, mix_norm, w_in, q_latent_norm, w_q_up, kv_latent_norm, w_kv_up, q_head_norm, k_head_norm, w_branch_mla, w_branch_sb, w_out, ffn2_norm, ffn2_w_in, ffn2_w_out, ple_norm, w_ple_gate, w_ple_proj):
    b, s, _ = x.shape
    cosf, sinf = _rope_tables(positions)
    weights = (ffn1_norm, ffn1_w_in, ffn1_w_out, mix_norm, w_in, q_latent_norm, w_q_up, kv_latent_norm, w_kv_up,
               q_head_norm, k_head_norm, w_branch_mla, w_branch_sb, w_out, ffn2_norm, ffn2_w_in, ffn2_w_out,
               ple_norm, w_ple_gate, w_ple_proj)
    h = x.reshape(b * s, D_MODEL)
    for i in range(p.shape[0]):
        h = _layer(h, p[i].reshape(b * s, -1), cosf, sinf, b, s, *(w[i] for w in weights))
    return h.reshape(b, s, D_MODEL)
```

```python
import functools
import math

import jax
import jax.numpy as jnp
from jax import lax
from jax.experimental import pallas as pl
from jax.experimental.pallas import tpu as pltpu

D_MODEL = 1024
D_FF = 2816
FFN_RES_WEIGHT = 0.5
NORM_EPS = 1e-6

MLA_HEADS = 8
MLA_NOPE = 64
MLA_ROPE = 32
MLA_QK = MLA_NOPE + MLA_ROPE
MLA_V = 64
Q_LORA = 384
KV_LORA = 256
ROPE_BASE = 10000.0

SB_HEADS = 8
SB_HEAD_DIM = 64
SB_WIDTH = SB_HEADS * SB_HEAD_DIM
MLA_WIDTH = MLA_HEADS * MLA_V
V_HEAD = 64

LANES = 128
HEAD_PAD = LANES
MLA_QK_PAD = MLA_HEADS * HEAD_PAD
HALF_ROPE = MLA_ROPE // 2

VMEM_LIMIT_BYTES = 56 * 1024 * 1024
TOKEN_TILE = 512
ATTN_TILE = 256
FF_CHUNKS = ((0, 1024), (1024, 1024), (2048, 768))
MASKED_SCORE = -1e30
SB_DEAD_LOG2_WEIGHT = 151.0

BF16 = jnp.bfloat16
F32 = jnp.float32


def _dot(a, b):
    return jnp.dot(a, b, preferred_element_type=F32)


def _dot_nt(a, b):
    return lax.dot_general(a, b, (((1,), (1,)), ((), ())), preferred_element_type=F32)


def _rms(x, g, n=None):
    n = x.shape[-1] if n is None else n
    ms = jnp.sum(x * x, axis=-1, keepdims=True) * (1.0 / n)
    return x * lax.rsqrt(ms + NORM_EPS) * g


def _sigmoid(x):
    return 1.0 / (1.0 + jnp.exp(-x))


def _const_spec(shape):
    nd = len(shape)
    return pl.BlockSpec(shape, lambda *_: (0,) * nd, pipeline_mode=pl.Buffered(1))


def _row_spec(tile, width):
    return pl.BlockSpec((tile, width), lambda i: (i, 0))


def _params(n_axes):
    return pltpu.CompilerParams(dimension_semantics=("arbitrary",) * n_axes,
                                vmem_limit_bytes=VMEM_LIMIT_BYTES)


def _ffn_kernel(*refs, with_ple):
    if with_ple:
        h_ref, g_ref, win_ref, wout_ref, p_ref, pg_ref, wpg_ref, wpp_ref, o_ref = refs
    else:
        h_ref, g_ref, win_ref, wout_ref, o_ref = refs
    h = h_ref[...]
    u = _rms(h, g_ref[...]).astype(BF16)
    acc = None
    for c0, cw in FF_CHUNKS:
        a = _dot(u, win_ref[:, c0:c0 + cw])
        b = _dot(u, win_ref[:, D_FF + c0:D_FF + c0 + cw])
        hm = (a * _sigmoid(a) * b).astype(BF16)
        y = _dot(hm, wout_ref[c0:c0 + cw, :])
        acc = y if acc is None else acc + y
    h = h + FFN_RES_WEIGHT * acc
    if with_ple:
        gate = _sigmoid(_dot(_rms(h, pg_ref[...]).astype(BF16), wpg_ref[...]))
        h = h + gate * _dot(p_ref[...].astype(BF16), wpp_ref[...])
    o_ref[...] = h


def _ffn(h, g, w_in, w_out, ple=None):
    t = h.shape[0]
    tm = min(TOKEN_TILE, t)
    ins = [h, g, w_in, w_out]
    specs = [_row_spec(tm, D_MODEL), _const_spec(g.shape), _const_spec(w_in.shape), _const_spec(w_out.shape)]
    if ple is not None:
        p, pg, wpg, wpp = ple
        ins += [p, pg, wpg, wpp]
        specs += [_row_spec(tm, p.shape[1]), _const_spec(pg.shape), _const_spec(wpg.shape), _const_spec(wpp.shape)]
    return pl.pallas_call(
        functools.partial(_ffn_kernel, with_ple=ple is not None),
        out_shape=jax.ShapeDtypeStruct(h.shape, F32),
        grid=(t // tm,),
        in_specs=specs,
        out_specs=_row_spec(tm, D_MODEL),
        compiler_params=_params(1),
        name="ffn_ple" if ple is not None else "ffn",
    )(*ins)


def _rope_partner(x):
    lane = lax.broadcasted_iota(jnp.int32, x.shape, 1)
    return jnp.where(lane < MLA_NOPE + HALF_ROPE,
                     pltpu.roll(x, HEAD_PAD - HALF_ROPE, 1), pltpu.roll(x, HALF_ROPE, 1))


def _head_norm_rope(x, g, cosf, sinf):
    x = _rms(x, g, MLA_QK)
    return x * cosf + _rope_partner(x) * sinf


def _store_key_blocks_transposed(vt_ref, v, tile):
    for c in range(v.shape[0] // tile):
        vt_ref[c] = v[c * tile:(c + 1) * tile, :].T.astype(BF16)


def _mix_proj_kernel(h_ref, gmix_ref, wlat_ref, wsb_ref, gq_ref, wq_ref, gkv_ref, wkv_ref, gqh_ref, gkh_ref,
                     cos_ref, sin_ref, q_ref, k_ref, vt_ref, sq_ref, sk_ref, svt_ref, *, tile):
    u = _rms(h_ref[...], gmix_ref[...]).astype(BF16)
    lat = _dot(u, wlat_ref[...])
    sb = _dot(u, wsb_ref[...])
    scale_sb = 1.0 / math.sqrt(SB_HEAD_DIM)
    sq_ref[...] = (sb[:, :SB_WIDTH] * scale_sb).astype(BF16)
    sk_ref[...] = sb[:, SB_WIDTH:2 * SB_WIDTH].astype(BF16)
    _store_key_blocks_transposed(svt_ref, sb[:, 2 * SB_WIDTH:], tile)

    c_q = lat[:, :Q_LORA]
    c_kv = lat[:, Q_LORA:Q_LORA + KV_LORA]
    k_rope = lat[:, Q_LORA + KV_LORA:]
    q = _dot(_rms(c_q, gq_ref[...]).astype(BF16), wq_ref[...])
    kv = _dot(_rms(c_kv, gkv_ref[...]).astype(BF16), wkv_ref[...])
    _store_key_blocks_transposed(vt_ref, kv[:, MLA_QK_PAD:], tile)
    cosf, sinf = cos_ref[...], sin_ref[...]
    gqh, gkh = gqh_ref[...], gkh_ref[...]
    for hd in range(MLA_HEADS):
        sl = slice(hd * HEAD_PAD, (hd + 1) * HEAD_PAD)
        q_ref[:, sl] = _head_norm_rope(q[:, sl], gqh, cosf, sinf).astype(BF16)
        k_ref[:, sl] = _head_norm_rope(kv[:, sl] + k_rope, gkh, cosf, sinf).astype(BF16)


def _mix_proj(h, gmix, wlat, wsb, gq, wq, gkv, wkv, gqh, gkh, cosf, sinf, tile):
    t = h.shape[0]
    tm = min(TOKEN_TILE, t)
    consts = [gmix, wlat, wsb, gq, wq, gkv, wkv, gqh, gkh]
    rows = lambda w: (jax.ShapeDtypeStruct((t, w), BF16), _row_spec(tm, w))
    key_blocks_t = lambda w: (jax.ShapeDtypeStruct((t // tile, w, tile), BF16),
                              pl.BlockSpec((tm // tile, w, tile), lambda i: (i, 0, 0)))
    outs = [rows(MLA_QK_PAD), rows(MLA_QK_PAD), key_blocks_t(MLA_WIDTH),
            rows(SB_WIDTH), rows(SB_WIDTH), key_blocks_t(SB_WIDTH)]
    return pl.pallas_call(
        functools.partial(_mix_proj_kernel, tile=tile),
        out_shape=[o[0] for o in outs],
        grid=(t // tm,),
        in_specs=[_row_spec(tm, D_MODEL)] + [_const_spec(c.shape) for c in consts]
                 + [_row_spec(tm, HEAD_PAD), _row_spec(tm, HEAD_PAD)],
        out_specs=[o[1] for o in outs],
        compiler_params=_params(1),
        name="mix_proj",
    )(h, *consts, cosf, sinf)


def _tile_iotas(tile):
    key = lax.broadcasted_iota(jnp.int32, (tile, tile), 0)
    qry = lax.broadcasted_iota(jnp.int32, (tile, tile), 1)
    return key, qry


def _key_rows(j, tile):
    return pl.ds(pl.multiple_of(j * tile, tile), tile)


def _mla_attn_kernel(q_ref, k_ref, vt_ref, o_ref, s_scr, p_scr, bias_scr, *, tile, n_tiles):
    c = math.log2(math.e) / math.sqrt(MLA_QK)
    heads = [slice(hd * HEAD_PAD, (hd + 1) * HEAD_PAD) for hd in range(2)]
    n_steps = n_tiles * (n_tiles + 1) // 2

    def scores(i, j, slot):
        q_rows, k_rows = _key_rows(i, tile), _key_rows(j, tile)
        for hd in range(2):
            s_scr[slot, hd] = _dot_nt(k_ref[0, k_rows, heads[hd]], q_ref[0, q_rows, heads[hd]])

    def weighted_values(j, slot, alphas, accs):
        return tuple(alphas[hd] * accs[hd] + _dot(vt_ref[0, j, hd * V_HEAD:(hd + 1) * V_HEAD, :], p_scr[slot, hd])
                     for hd in range(2))

    def step(slot, carry):
        i, j, ms, ls, alphas, accs = carry
        on_diag = j == i
        next_i, next_j = jnp.where(on_diag, i + 1, i), jnp.where(on_diag, 0, j + 1)
        scores(jnp.minimum(next_i, n_tiles - 1), next_j, 1 - slot)
        prev_j = jnp.maximum(jnp.where(j > 0, j - 1, i - 1), 0)
        accs = weighted_values(prev_j, 1 - slot, alphas, accs)
        new_tile = jnp.logical_and(j == 0, i > 0)
        done = (new_tile, jnp.maximum(i - 1, 0), ls, accs)
        keep = jnp.where(new_tile, 0.0, 1.0)
        ms = tuple(m * keep + MASKED_SCORE * (1.0 - keep) for m in ms)
        ls = tuple(l * keep for l in ls)
        accs = tuple(acc * keep for acc in accs)
        bias = bias_scr[on_diag.astype(jnp.int32)]
        out = []
        for hd in range(2):
            s = s_scr[slot, hd] + bias
            m_new = jnp.maximum(ms[hd], jnp.max(s, axis=0, keepdims=True))
            alpha = jnp.exp2((ms[hd] - m_new) * c)
            p = jnp.exp2((s - m_new) * c)
            p_scr[slot, hd] = p.astype(BF16)
            out.append((m_new, alpha * ls[hd] + jnp.sum(p, axis=0, keepdims=True), alpha))
        ms, ls, alphas = zip(*out)
        return (next_i, next_j, ms, ls, alphas, accs), done

    def store_tile(i, ls, accs):
        o_t = jnp.concatenate([accs[hd] / ls[hd] for hd in range(2)], axis=0)
        o_ref[0, _key_rows(i, tile), :] = o_t.T.astype(BF16)

    def store_if_done(done):
        flag, i, ls, accs = done
        pl.when(flag)(lambda: store_tile(i, ls, accs))

    key, qry = _tile_iotas(tile)
    bias_scr[0] = jnp.zeros((tile, tile), F32)
    bias_scr[1] = jnp.where(key <= qry, 0.0, MASKED_SCORE)
    p_scr[1] = jnp.zeros(p_scr.shape[1:], BF16)
    scores(0, 0, 0)
    row = lambda v: (jnp.full((1, tile), v, F32),) * 2
    carry = (jnp.int32(0), jnp.int32(0), row(MASKED_SCORE), row(0.0), row(1.0), (jnp.zeros((V_HEAD, tile), F32),) * 2)
    slots = (0, 1)
    if n_steps % 2:
        carry, done = step(0, carry)
        store_if_done(done)
        slots = (1, 0)

    def two_steps(_, carry):
        carry, done_a = step(slots[0], carry)
        carry, done_b = step(slots[1], carry)
        store_if_done(done_a)
        store_if_done(done_b)
        return carry

    _, _, _, ls, alphas, accs = lax.fori_loop(0, n_steps // 2, two_steps, carry)
    accs = weighted_values(n_tiles - 1, (n_steps - 1) % 2, alphas, accs)
    store_tile(n_tiles - 1, ls, accs)


def _sb_attn_kernel(q_ref, k_ref, vt_ref, o_ref, *, tile, n_tiles):
    log2e = math.log2(math.e)
    key, qry = _tile_iotas(tile)
    strict = key < qry
    from_key = (qry >= key).astype(BF16)
    low = lax.broadcasted_iota(jnp.int32, (tile, LANES), 1) < V_HEAD

    def query_tile(i, _):
        q_pair = q_ref[0, _key_rows(i, tile), :]
        qs = [jnp.where(low, q_pair, jnp.zeros_like(q_pair)), jnp.where(low, jnp.zeros_like(q_pair), q_pair)]

        def two_blocks(j, deads, accs, masked):
            has_second = j >= 1
            idx = (j, jnp.maximum(j - 1, 0))
            k_blks = [k_ref[0, _key_rows(jb, tile), :] for jb in idx]
            z2s = [[_dot_nt(k_blks[b], qs[hd]) * log2e for hd in range(2)] for b in range(2)]
            incl = [[None, None], [None, None]]
            for b in range(2):
                for hd in range(2):
                    z2 = z2s[b][hd]
                    x = jnp.maximum(z2, 0.0) + jnp.log2(1.0 + jnp.exp2(-jnp.abs(z2)))
                    if masked and b == 0:
                        x = jnp.where(strict, x, 0.0)
                    hi = x.astype(BF16)
                    lo = (x - hi.astype(F32)).astype(BF16)
                    incl[b][hd] = _dot(from_key, hi) + _dot(from_key, lo)
            new_deads, new_accs = [], []
            for hd in range(2):
                dead = [deads[hd], jnp.where(has_second, deads[hd] + incl[0][hd][0:1, :], jnp.inf)]
                acc = accs[hd]
                for b in range(2):
                    a = jnp.exp2(z2s[b][hd] - incl[b][hd] - dead[b])
                    if masked and b == 0:
                        a = jnp.where(strict, a, 0.0)
                    acc = acc + _dot(vt_ref[0, idx[b], hd * V_HEAD:(hd + 1) * V_HEAD, :], a.astype(BF16))
                new_deads.append(dead[1] + incl[1][hd][0:1, :])
                new_accs.append(acc)
            return tuple(new_deads), tuple(new_accs)

        def alive(deads):
            return (jnp.minimum(jnp.min(deads[0]), jnp.min(deads[1])) <= SB_DEAD_LOG2_WEIGHT).astype(jnp.int32)

        deads, accs = two_blocks(i, (jnp.zeros((1, tile), F32),) * 2, (jnp.zeros((V_HEAD, tile), F32),) * 2, True)

        def more(carry):
            j, live, _, _ = carry
            return jnp.logical_and(j >= 0, live > 0)

        def body(carry):
            j, _, deads, accs = carry
            deads, accs = two_blocks(j, deads, accs, False)
            return j - 2, alive(deads), deads, accs

        _, _, _, accs = lax.while_loop(more, body, (i - 2, alive(deads), deads, accs))
        o_t = jnp.concatenate(accs, axis=0)
        o_ref[0, _key_rows(i, tile), :] = o_t.T.astype(BF16)
        return 0

    lax.fori_loop(0, n_tiles, query_tile, 0)


def _mla_attention(q, k, vt, tile):
    b, s, _ = q.shape
    n_tiles = s // tile
    seq_pair = pl.BlockSpec((1, s, 2 * HEAD_PAD), lambda b, p: (b, 0, p))
    return pl.pallas_call(
        functools.partial(_mla_attn_kernel, tile=tile, n_tiles=n_tiles),
        out_shape=jax.ShapeDtypeStruct((b, s, MLA_WIDTH), BF16),
        grid=(b, MLA_HEADS // 2),
        in_specs=[seq_pair, seq_pair, pl.BlockSpec((1, n_tiles, LANES, tile), lambda b, p: (b, 0, p, 0))],
        out_specs=pl.BlockSpec((1, s, LANES), lambda b, p: (b, 0, p)),
        scratch_shapes=[pltpu.VMEM((2, 2, tile, tile), F32), pltpu.VMEM((2, 2, tile, tile), BF16),
                        pltpu.VMEM((2, tile, tile), F32)],
        compiler_params=_params(2),
        name="mla_attn",
    )(q, k, vt)


def _sb_attention(q, k, vt, tile):
    b, s, _ = q.shape
    n_tiles = s // tile
    seq_pair = pl.BlockSpec((1, s, LANES), lambda b, p: (b, 0, p))
    return pl.pallas_call(
        functools.partial(_sb_attn_kernel, tile=tile, n_tiles=n_tiles),
        out_shape=jax.ShapeDtypeStruct((b, s, SB_WIDTH), BF16),
        grid=(b, SB_HEADS // 2),
        in_specs=[seq_pair, seq_pair, pl.BlockSpec((1, n_tiles, LANES, tile), lambda b, p: (b, 0, p, 0))],
        out_specs=seq_pair,
        compiler_params=_params(2),
        name="sb_attn",
    )(q, k, vt)


def _merge_kernel(h_ref, gmix_ref, wg_ref, om_ref, os_ref, wbm_ref, wbs_ref, wo_ref, o_ref):
    h = h_ref[...]
    u = _rms(h, gmix_ref[...]).astype(BF16)
    gates = _sigmoid(_dot(u, wg_ref[...]))
    merged = gates[:, :D_MODEL] * _dot(om_ref[...], wbm_ref[...]) + gates[:, D_MODEL:] * _dot(os_ref[...], wbs_ref[...])
    o_ref[...] = h + _dot(merged.astype(BF16), wo_ref[...])


def _merge(h, gmix, wg, om, osb, wbm, wbs, wo):
    t = h.shape[0]
    tm = min(TOKEN_TILE, t)
    return pl.pallas_call(
        _merge_kernel,
        out_shape=jax.ShapeDtypeStruct(h.shape, F32),
        grid=(t // tm,),
        in_specs=[_row_spec(tm, D_MODEL), _const_spec(gmix.shape), _const_spec(wg.shape),
                  _row_spec(tm, MLA_WIDTH), _row_spec(tm, SB_WIDTH),
                  _const_spec(wbm.shape), _const_spec(wbs.shape), _const_spec(wo.shape)],
        out_specs=_row_spec(tm, D_MODEL),
        compiler_params=_params(1),
        name="merge",
    )(h, gmix, wg, om, osb, wbm, wbs, wo)


def _pad_heads(w, head_dim, real):
    lead = w.shape[:-1]
    w = w.reshape(lead + (-1, head_dim))[..., :real]
    w = jnp.pad(w, [(0, 0)] * len(lead) + [(0, 0), (0, HEAD_PAD - real)])
    return w.reshape(lead + (-1,))


def _rope_tables(positions):
    inv_freq = ROPE_BASE ** (-jnp.arange(0, MLA_ROPE, 2, dtype=F32) / MLA_ROPE)
    ang = positions.astype(F32).reshape(-1, 1) * inv_freq
    cos, sin = jnp.cos(ang), jnp.sin(ang)
    t = ang.shape[0]
    tail = HEAD_PAD - MLA_QK
    cosf = jnp.concatenate([jnp.ones((t, MLA_NOPE), F32), cos, cos, jnp.ones((t, tail), F32)], axis=-1)
    sinf = jnp.concatenate([jnp.zeros((t, MLA_NOPE), F32), -sin, sin, jnp.zeros((t, tail), F32)], axis=-1)
    return cosf, sinf


def _layer(h, p, cosf, sinf, b, s, ffn1_norm, ffn1_w_in, ffn1_w_out, mix_norm, w_in, q_latent_norm, w_q_up,
           kv_latent_norm, w_kv_up, q_head_norm, k_head_norm, w_branch_mla, w_branch_sb, w_out, ffn2_norm,
           ffn2_w_in, ffn2_w_out, ple_norm, w_ple_gate, w_ple_proj):
    row = lambda g: g.reshape(1, -1)
    c0, c1, c2, c3 = Q_LORA, Q_LORA + KV_LORA, Q_LORA + KV_LORA + MLA_ROPE, Q_LORA + KV_LORA + MLA_ROPE + 3 * SB_WIDTH
    w_krope = jnp.pad(w_in[:, c1:c2], ((0, 0), (MLA_NOPE, HEAD_PAD - MLA_QK)))
    w_lat = jnp.concatenate([w_in[:, :c1], w_krope], axis=1).astype(BF16)
    w_sb = w_in[:, c2:c3].astype(BF16)
    w_gates = w_in[:, c3:].astype(BF16)
    w_q = _pad_heads(w_q_up, MLA_QK, MLA_QK).astype(BF16)
    kv_heads = w_kv_up.reshape(KV_LORA, MLA_HEADS, MLA_NOPE + MLA_V)
    w_knope = _pad_heads(kv_heads[..., :MLA_NOPE].reshape(KV_LORA, -1), MLA_NOPE, MLA_NOPE)
    w_kv = jnp.concatenate([w_knope, kv_heads[..., MLA_NOPE:].reshape(KV_LORA, -1)], axis=1).astype(BF16)
    gqh = jnp.pad(q_head_norm, (0, HEAD_PAD - MLA_QK)).reshape(1, -1)
    gkh = jnp.pad(k_head_norm, (0, HEAD_PAD - MLA_QK)).reshape(1, -1)
    tile = min(ATTN_TILE, s)

    h = _ffn(h, row(ffn1_norm), ffn1_w_in.astype(BF16), ffn1_w_out.astype(BF16))
    q, k, vt, sq, sk, svt = _mix_proj(h, row(mix_norm), w_lat, w_sb, row(q_latent_norm), w_q, row(kv_latent_norm),
                                      w_kv, gqh, gkh, cosf, sinf, tile)
    seq = lambda a: a.reshape(b, s, a.shape[-1])
    key_blocks = lambda a: a.reshape(b, s // tile, a.shape[1], tile)
    o_mla = _mla_attention(seq(q), seq(k), key_blocks(vt), tile)
    o_sb = _sb_attention(seq(sq), seq(sk), key_blocks(svt), tile)
    h = _merge(h, row(mix_norm), w_gates, o_mla.reshape(b * s, -1), o_sb.reshape(b * s, -1),
               w_branch_mla.astype(BF16), w_branch_sb.astype(BF16), w_out.astype(BF16))
    return _ffn(h, row(ffn2_norm), ffn2_w_in.astype(BF16), ffn2_w_out.astype(BF16),
                ple=(p, row(ple_norm), w_ple_gate.astype(BF16), w_ple_proj.astype(BF16)))


def kernel(x, p, positions, ffn1_norm, ffn1_w_in, ffn1_w_out, mix_norm, w_in, q_latent_norm, w_q_up, kv_latent_norm, w_kv_up, q_head_norm, k_head_norm, w_branch_mla, w_branch_sb, w_out, ffn2_norm, ffn2_w_in, ffn2_w_out, ple_norm, w_ple_gate, w_ple_proj):
    b, s, _ = x.shape
    cosf, sinf = _rope_tables(positions)
    weights = (ffn1_norm, ffn1_w_in, ffn1_w_out, mix_norm, w_in, q_latent_norm, w_q_up, kv_latent_norm, w_kv_up,
               q_head_norm, k_head_norm, w_branch_mla, w_branch_sb, w_out, ffn2_norm, ffn2_w_in, ffn2_w_out,
               ple_norm, w_ple_gate, w_ple_proj)
    h = x.reshape(b * s, D_MODEL)
    for i in range(p.shape[0]):
        h = _layer(h, p[i].reshape(b * s, -1), cosf, sinf, b, s, *(w[i] for w in weights))
    return h.reshape(b, s, D_MODEL)
```

```python
import functools
import math

import jax
import jax.numpy as jnp
from jax import lax
from jax.experimental import pallas as pl
from jax.experimental.pallas import tpu as pltpu

D_MODEL = 1024
D_FF = 2816
FFN_RES_WEIGHT = 0.5
NORM_EPS = 1e-6

MLA_HEADS = 8
MLA_NOPE = 64
MLA_ROPE = 32
MLA_QK = MLA_NOPE + MLA_ROPE
MLA_V = 64
Q_LORA = 384
KV_LORA = 256
ROPE_BASE = 10000.0

SB_HEADS = 8
SB_HEAD_DIM = 64
SB_WIDTH = SB_HEADS * SB_HEAD_DIM
MLA_WIDTH = MLA_HEADS * MLA_V
V_HEAD = 64

LANES = 128
HEAD_PAD = LANES
MLA_QK_PAD = MLA_HEADS * HEAD_PAD
HALF_ROPE = MLA_ROPE // 2

VMEM_LIMIT_BYTES = 56 * 1024 * 1024
TOKEN_TILE = 512
ATTN_TILE = 256
FF_CHUNKS = ((0, 1024), (1024, 1024), (2048, 768))
MASKED_SCORE = -1e30
SB_DEAD_LOG2_WEIGHT = 151.0

BF16 = jnp.bfloat16
F32 = jnp.float32


def _dot(a, b):
    return jnp.dot(a, b, preferred_element_type=F32)


def _rms(x, g, n=None):
    n = x.shape[-1] if n is None else n
    ms = jnp.sum(x * x, axis=-1, keepdims=True) * (1.0 / n)
    return x * lax.rsqrt(ms + NORM_EPS) * g


def _sigmoid(x):
    return 1.0 / (1.0 + jnp.exp(-x))


def _const_spec(shape):
    nd = len(shape)
    return pl.BlockSpec(shape, lambda *_: (0,) * nd, pipeline_mode=pl.Buffered(1))


def _row_spec(tile, width):
    return pl.BlockSpec((tile, width), lambda i: (i, 0))


def _params(n_axes):
    return pltpu.CompilerParams(dimension_semantics=("arbitrary",) * n_axes,
                                vmem_limit_bytes=VMEM_LIMIT_BYTES)


def _ffn_kernel(*refs, with_ple):
    if with_ple:
        h_ref, g_ref, win_ref, wout_ref, p_ref, pg_ref, wpg_ref, wpp_ref, o_ref = refs
    else:
        h_ref, g_ref, win_ref, wout_ref, o_ref = refs
    h = h_ref[...]
    u = _rms(h, g_ref[...]).astype(BF16)
    acc = None
    for c0, cw in FF_CHUNKS:
        a = _dot(u, win_ref[:, c0:c0 + cw])
        b = _dot(u, win_ref[:, D_FF + c0:D_FF + c0 + cw])
        hm = (a * _sigmoid(a) * b).astype(BF16)
        y = _dot(hm, wout_ref[c0:c0 + cw, :])
        acc = y if acc is None else acc + y
    h = h + FFN_RES_WEIGHT * acc
    if with_ple:
        gate = _sigmoid(_dot(_rms(h, pg_ref[...]).astype(BF16), wpg_ref[...]))
        h = h + gate * _dot(p_ref[...].astype(BF16), wpp_ref[...])
    o_ref[...] = h


def _ffn(h, g, w_in, w_out, ple=None):
    t = h.shape[0]
    tm = min(TOKEN_TILE, t)
    ins = [h, g, w_in, w_out]
    specs = [_row_spec(tm, D_MODEL), _const_spec(g.shape), _const_spec(w_in.shape), _const_spec(w_out.shape)]
    if ple is not None:
        p, pg, wpg, wpp = ple
        ins += [p, pg, wpg, wpp]
        specs += [_row_spec(tm, p.shape[1]), _const_spec(pg.shape), _const_spec(wpg.shape), _const_spec(wpp.shape)]
    return pl.pallas_call(
        functools.partial(_ffn_kernel, with_ple=ple is not None),
        out_shape=jax.ShapeDtypeStruct(h.shape, F32),
        grid=(t // tm,),
        in_specs=specs,
        out_specs=_row_spec(tm, D_MODEL),
        compiler_params=_params(1),
        name="ffn_ple" if ple is not None else "ffn",
    )(*ins)


def _head_norm_rope_t(xt, g_col, cos_t, sin_t):
    ms = jnp.sum(xt * xt, axis=0, keepdims=True) * (1.0 / MLA_QK)
    xt = xt * lax.rsqrt(ms + NORM_EPS) * g_col
    x1, x2 = xt[MLA_NOPE:MLA_NOPE + HALF_ROPE], xt[MLA_NOPE + HALF_ROPE:MLA_QK]
    return jnp.concatenate([xt[:MLA_NOPE], x1 * cos_t - x2 * sin_t, x2 * cos_t + x1 * sin_t, xt[MLA_QK:]], axis=0)


def _store_blocks_transposed(t_ref, v, tile, col0=0, finish=lambda c, vt: vt):
    for c in range(v.shape[0] // tile):
        t_ref[c, col0:col0 + v.shape[1], :] = finish(c, v[c * tile:(c + 1) * tile, :].T).astype(BF16)


def _rows_via_transposed(v, tile, finish):
    return jnp.concatenate([finish(c, v[c * tile:(c + 1) * tile, :].T).T for c in range(v.shape[0] // tile)], axis=0)


def _mix_proj_kernel(h_ref, gmix_ref, wlat_ref, wsb_ref, gq_ref, wq_ref, gkv_ref, wkv_ref, gqh_ref, gkh_ref,
                     cost_ref, sint_ref, qt_ref, k_ref, vt_ref, sqt_ref, sk_ref, svt_ref, *, tile):
    u = _rms(h_ref[...], gmix_ref[...]).astype(BF16)
    lat = _dot(u, wlat_ref[...])
    sb = _dot(u, wsb_ref[...])
    scale_sb = 1.0 / math.sqrt(SB_HEAD_DIM)
    _store_blocks_transposed(sqt_ref, sb[:, :SB_WIDTH] * scale_sb, tile)
    sk_ref[...] = sb[:, SB_WIDTH:2 * SB_WIDTH].astype(BF16)
    _store_blocks_transposed(svt_ref, sb[:, 2 * SB_WIDTH:], tile)

    c_q = lat[:, :Q_LORA]
    c_kv = lat[:, Q_LORA:Q_LORA + KV_LORA]
    k_rope = lat[:, Q_LORA + KV_LORA:]
    q = _dot(_rms(c_q, gq_ref[...]).astype(BF16), wq_ref[...])
    kv = _dot(_rms(c_kv, gkv_ref[...]).astype(BF16), wkv_ref[...])
    _store_blocks_transposed(vt_ref, kv[:, MLA_QK_PAD:], tile)
    gqh_col, gkh_col = gqh_ref[...], gkh_ref[...]
    q_finish = lambda c, xt: _head_norm_rope_t(xt, gqh_col, cost_ref[c], sint_ref[c])
    k_finish = lambda c, xt: _head_norm_rope_t(xt, gkh_col, cost_ref[c], sint_ref[c])
    for hd in range(MLA_HEADS):
        sl = slice(hd * HEAD_PAD, (hd + 1) * HEAD_PAD)
        _store_blocks_transposed(qt_ref, q[:, sl], tile, hd * HEAD_PAD, q_finish)
        k_ref[:, sl] = _rows_via_transposed(kv[:, sl] + k_rope, tile, k_finish).astype(BF16)


def _mix_proj(h, gmix, wlat, wsb, gq, wq, gkv, wkv, gqh_col, gkh_col, rope, tile):
    t = h.shape[0]
    tm = min(TOKEN_TILE, t)
    cos_t, sin_t = rope
    consts = [gmix, wlat, wsb, gq, wq, gkv, wkv, gqh_col, gkh_col]
    angle_blocks = pl.BlockSpec((tm // tile, HALF_ROPE, tile), lambda i: (i, 0, 0))
    rows = lambda w: (jax.ShapeDtypeStruct((t, w), BF16), _row_spec(tm, w))
    blocks_t = lambda w: (jax.ShapeDtypeStruct((t // tile, w, tile), BF16),
                          pl.BlockSpec((tm // tile, w, tile), lambda i: (i, 0, 0)))
    outs = [blocks_t(MLA_QK_PAD), rows(MLA_QK_PAD), blocks_t(MLA_WIDTH),
            blocks_t(SB_WIDTH), rows(SB_WIDTH), blocks_t(SB_WIDTH)]
    return pl.pallas_call(
        functools.partial(_mix_proj_kernel, tile=tile),
        out_shape=[o[0] for o in outs],
        grid=(t // tm,),
        in_specs=[_row_spec(tm, D_MODEL)] + [_const_spec(c.shape) for c in consts]
                 + [angle_blocks, angle_blocks],
        out_specs=[o[1] for o in outs],
        compiler_params=_params(1),
        name="mix_proj",
    )(h, *consts, cos_t, sin_t)


def _tile_iotas(tile):
    key = lax.broadcasted_iota(jnp.int32, (tile, tile), 0)
    qry = lax.broadcasted_iota(jnp.int32, (tile, tile), 1)
    return key, qry


def _key_rows(j, tile):
    return pl.ds(pl.multiple_of(j * tile, tile), tile)


def _mla_attn_kernel(qt_ref, k_ref, vt_ref, o_ref, s_scr, p_scr, bias_scr, *, tile, n_tiles):
    c = math.log2(math.e) / math.sqrt(MLA_QK)
    heads = [slice(hd * HEAD_PAD, (hd + 1) * HEAD_PAD) for hd in range(2)]
    n_steps = n_tiles * (n_tiles + 1) // 2

    def scores(i, j, slot, may_be_diagonal):
        k_rows = _key_rows(j, tile)
        tops = []
        for hd in range(2):
            s = _dot(k_ref[0, k_rows, heads[hd]], qt_ref[0, i, heads[hd], :])
            if may_be_diagonal:
                s = s + bias_scr[(i == j).astype(jnp.int32)]
            s_scr[slot, hd] = s
            tops.append(jnp.max(s, axis=0, keepdims=True))
        return tuple(tops)

    def weighted_values(j, slot, alphas, accs):
        return tuple(alphas[hd] * accs[hd] + _dot(vt_ref[0, j, hd * V_HEAD:(hd + 1) * V_HEAD, :], p_scr[slot, hd])
                     for hd in range(2))

    def step(slot, carry, may_be_diagonal=True):
        i, j, tops, ms, ls, alphas, accs = carry
        on_diag = j == i
        next_i, next_j = jnp.where(on_diag, i + 1, i), jnp.where(on_diag, 0, j + 1)
        next_tops = scores(jnp.minimum(next_i, n_tiles - 1), next_j, 1 - slot, may_be_diagonal)
        prev_j = jnp.maximum(jnp.where(j > 0, j - 1, i - 1), 0)
        accs = weighted_values(prev_j, 1 - slot, alphas, accs)
        new_tile = jnp.logical_and(j == 0, i > 0)
        done = (new_tile, jnp.maximum(i - 1, 0), ls, accs)
        keep = jnp.where(new_tile, 0.0, 1.0)
        ms = tuple(m * keep + MASKED_SCORE * (1.0 - keep) for m in ms)
        out = []
        for hd in range(2):
            m_new = jnp.maximum(ms[hd], tops[hd])
            alpha = jnp.exp2((ms[hd] - m_new) * c)
            p = jnp.exp2((s_scr[slot, hd] - m_new) * c)
            p_scr[slot, hd] = p.astype(BF16)
            out.append((m_new, alpha * ls[hd] + jnp.sum(p, axis=0, keepdims=True), alpha))
        ms, ls, alphas = zip(*out)
        return (next_i, next_j, next_tops, ms, ls, alphas, accs), done

    def store_tile(i, ls, accs):
        o_t = jnp.concatenate([accs[hd] / ls[hd] for hd in range(2)], axis=0)
        o_ref[0, _key_rows(i, tile), :] = o_t.T.astype(BF16)

    def store_if_done(done):
        flag, i, ls, accs = done
        pl.when(flag)(lambda: store_tile(i, ls, accs))

    key, qry = _tile_iotas(tile)
    bias_scr[0] = jnp.zeros((tile, tile), F32)
    bias_scr[1] = jnp.where(key <= qry, 0.0, MASKED_SCORE)
    p_scr[1] = jnp.zeros(p_scr.shape[1:], BF16)
    zero = jnp.int32(0)
    tops = scores(zero, zero, 0, True)
    row = lambda v: (jnp.full((1, tile), v, F32),) * 2
    carry = (zero, zero, tops, row(MASKED_SCORE), row(0.0), row(1.0), (jnp.zeros((V_HEAD, tile), F32),) * 2)
    slots = (0, 1)
    if n_steps % 2:
        carry, done = step(0, carry)
        store_if_done(done)
        slots = (1, 0)

    def two_steps(carry, may_be_diagonal):
        carry, done_a = step(slots[0], carry, may_be_diagonal)
        carry, done_b = step(slots[1], carry, may_be_diagonal)
        store_if_done(done_a)
        store_if_done(done_b)
        return carry

    def trip(_, carry):
        i, j = carry[0], carry[1]
        gap = i - j
        diagonal_ahead = jnp.logical_or(jnp.logical_or(gap == 1, gap == 2), i == 0)
        return lax.cond(diagonal_ahead, functools.partial(two_steps, may_be_diagonal=True),
                        functools.partial(two_steps, may_be_diagonal=False), carry)

    _, _, _, _, ls, alphas, accs = lax.fori_loop(0, n_steps // 2, trip, carry)
    accs = weighted_values(n_tiles - 1, (n_steps - 1) % 2, alphas, accs)
    store_tile(n_tiles - 1, ls, accs)


def _sb_attn_kernel(qt_ref, k_ref, vt_ref, o_ref, *, tile, n_tiles):
    log2e = math.log2(math.e)
    key, qry = _tile_iotas(tile)
    strict = key < qry
    from_key = (qry >= key).astype(BF16)
    low = lax.broadcasted_iota(jnp.int32, (LANES, tile), 0) < V_HEAD

    def query_tile(i, _):
        q_pair = qt_ref[0, i]
        qs = [jnp.where(low, q_pair, jnp.zeros_like(q_pair)), jnp.where(low, jnp.zeros_like(q_pair), q_pair)]

        def two_blocks(j, deads, accs, masked):
            has_second = j >= 1
            idx = (j, jnp.maximum(j - 1, 0))
            k_blks = [k_ref[0, _key_rows(jb, tile), :] for jb in idx]
            z2s = [[_dot(k_blks[b], qs[hd]) * log2e for hd in range(2)] for b in range(2)]
            incl = [[None, None], [None, None]]
            for b in range(2):
                for hd in range(2):
                    z2 = z2s[b][hd]
                    x = jnp.maximum(z2, 0.0) + jnp.log2(1.0 + jnp.exp2(-jnp.abs(z2)))
                    if masked and b == 0:
                        x = jnp.where(strict, x, 0.0)
                    hi = x.astype(BF16)
                    lo = (x - hi.astype(F32)).astype(BF16)
                    incl[b][hd] = _dot(from_key, hi) + _dot(from_key, lo)
            new_deads, new_accs = [], []
            for hd in range(2):
                dead = [deads[hd], jnp.where(has_second, deads[hd] + incl[0][hd][0:1, :], jnp.inf)]
                acc = accs[hd]
                for b in range(2):
                    a = jnp.exp2(z2s[b][hd] - incl[b][hd] - dead[b])
                    if masked and b == 0:
                        a = jnp.where(strict, a, 0.0)
                    acc = acc + _dot(vt_ref[0, idx[b], hd * V_HEAD:(hd + 1) * V_HEAD, :], a.astype(BF16))
                new_deads.append(dead[1] + incl[1][hd][0:1, :])
                new_accs.append(acc)
            return tuple(new_deads), tuple(new_accs)

        def alive(deads):
            return (jnp.minimum(jnp.min(deads[0]), jnp.min(deads[1])) <= SB_DEAD_LOG2_WEIGHT).astype(jnp.int32)

        deads, accs = two_blocks(i, (jnp.zeros((1, tile), F32),) * 2, (jnp.zeros((V_HEAD, tile), F32),) * 2, True)

        def more(carry):
            j, live, _, _ = carry
            return jnp.logical_and(j >= 0, live > 0)

        def body(carry):
            j, _, deads, accs = carry
            deads, accs = two_blocks(j, deads, accs, False)
            return j - 2, alive(deads), deads, accs

        _, _, _, accs = lax.while_loop(more, body, (i - 2, alive(deads), deads, accs))
        o_t = jnp.concatenate(accs, axis=0)
        o_ref[0, _key_rows(i, tile), :] = o_t.T.astype(BF16)
        return 0

    lax.fori_loop(0, n_tiles, query_tile, 0)


def _blocks_t_spec(n_tiles, rows, tile):
    return pl.BlockSpec((1, n_tiles, rows, tile), lambda b, p: (b, 0, p, 0))


def _mla_attention(qt, k, vt):
    b, n_tiles, _, tile = qt.shape
    s = n_tiles * tile
    return pl.pallas_call(
        functools.partial(_mla_attn_kernel, tile=tile, n_tiles=n_tiles),
        out_shape=jax.ShapeDtypeStruct((b, s, MLA_WIDTH), BF16),
        grid=(b, MLA_HEADS // 2),
        in_specs=[_blocks_t_spec(n_tiles, 2 * HEAD_PAD, tile),
                  pl.BlockSpec((1, s, 2 * HEAD_PAD), lambda b, p: (b, 0, p)),
                  _blocks_t_spec(n_tiles, LANES, tile)],
        out_specs=pl.BlockSpec((1, s, LANES), lambda b, p: (b, 0, p)),
        scratch_shapes=[pltpu.VMEM((2, 2, tile, tile), F32), pltpu.VMEM((2, 2, tile, tile), BF16),
                        pltpu.VMEM((2, tile, tile), F32)],
        compiler_params=_params(2),
        name="mla_attn",
    )(qt, k, vt)


def _sb_attention(qt, k, vt):
    b, n_tiles, _, tile = qt.shape
    s = n_tiles * tile
    seq_pair = pl.BlockSpec((1, s, LANES), lambda b, p: (b, 0, p))
    return pl.pallas_call(
        functools.partial(_sb_attn_kernel, tile=tile, n_tiles=n_tiles),
        out_shape=jax.ShapeDtypeStruct((b, s, SB_WIDTH), BF16),
        grid=(b, SB_HEADS // 2),
        in_specs=[_blocks_t_spec(n_tiles, LANES, tile), seq_pair, _blocks_t_spec(n_tiles, LANES, tile)],
        out_specs=seq_pair,
        compiler_params=_params(2),
        name="sb_attn",
    )(qt, k, vt)


def _merge_kernel(h_ref, gmix_ref, wg_ref, om_ref, os_ref, wbm_ref, wbs_ref, wo_ref, o_ref):
    h = h_ref[...]
    u = _rms(h, gmix_ref[...]).astype(BF16)
    gates = _sigmoid(_dot(u, wg_ref[...]))
    merged = gates[:, :D_MODEL] * _dot(om_ref[...], wbm_ref[...]) + gates[:, D_MODEL:] * _dot(os_ref[...], wbs_ref[...])
    o_ref[...] = h + _dot(merged.astype(BF16), wo_ref[...])


def _merge(h, gmix, wg, om, osb, wbm, wbs, wo):
    t = h.shape[0]
    tm = min(TOKEN_TILE, t)
    return pl.pallas_call(
        _merge_kernel,
        out_shape=jax.ShapeDtypeStruct(h.shape, F32),
        grid=(t // tm,),
        in_specs=[_row_spec(tm, D_MODEL), _const_spec(gmix.shape), _const_spec(wg.shape),
                  _row_spec(tm, MLA_WIDTH), _row_spec(tm, SB_WIDTH),
                  _const_spec(wbm.shape), _const_spec(wbs.shape), _const_spec(wo.shape)],
        out_specs=_row_spec(tm, D_MODEL),
        compiler_params=_params(1),
        name="merge",
    )(h, gmix, wg, om, osb, wbm, wbs, wo)


def _pad_heads(w, head_dim, real):
    lead = w.shape[:-1]
    w = w.reshape(lead + (-1, head_dim))[..., :real]
    w = jnp.pad(w, [(0, 0)] * len(lead) + [(0, 0), (0, HEAD_PAD - real)])
    return w.reshape(lead + (-1,))


def _rope_tables(positions, tile):
    inv_freq = ROPE_BASE ** (-jnp.arange(0, MLA_ROPE, 2, dtype=F32) / MLA_ROPE)
    ang_t = inv_freq.reshape(-1, 1) * positions.astype(F32).reshape(1, -1)
    blocks_t = lambda a: a.reshape(HALF_ROPE, -1, tile).transpose(1, 0, 2)
    return blocks_t(jnp.cos(ang_t)), blocks_t(jnp.sin(ang_t))


def _layer(h, p, rope, tile, b, s, ffn1_norm, ffn1_w_in, ffn1_w_out, mix_norm, w_in, q_latent_norm, w_q_up,
           kv_latent_norm, w_kv_up, q_head_norm, k_head_norm, w_branch_mla, w_branch_sb, w_out, ffn2_norm,
           ffn2_w_in, ffn2_w_out, ple_norm, w_ple_gate, w_ple_proj):
    row = lambda g: g.reshape(1, -1)
    c0, c1, c2, c3 = Q_LORA, Q_LORA + KV_LORA, Q_LORA + KV_LORA + MLA_ROPE, Q_LORA + KV_LORA + MLA_ROPE + 3 * SB_WIDTH
    w_krope = jnp.pad(w_in[:, c1:c2], ((0, 0), (MLA_NOPE, HEAD_PAD - MLA_QK)))
    w_lat = jnp.concatenate([w_in[:, :c1], w_krope], axis=1).astype(BF16)
    w_sb = w_in[:, c2:c3].astype(BF16)
    w_gates = w_in[:, c3:].astype(BF16)
    w_q = _pad_heads(w_q_up, MLA_QK, MLA_QK).astype(BF16)
    kv_heads = w_kv_up.reshape(KV_LORA, MLA_HEADS, MLA_NOPE + MLA_V)
    w_knope = _pad_heads(kv_heads[..., :MLA_NOPE].reshape(KV_LORA, -1), MLA_NOPE, MLA_NOPE)
    w_kv = jnp.concatenate([w_knope, kv_heads[..., MLA_NOPE:].reshape(KV_LORA, -1)], axis=1).astype(BF16)
    gqh_col = jnp.pad(q_head_norm, (0, HEAD_PAD - MLA_QK)).reshape(-1, 1)
    gkh_col = jnp.pad(k_head_norm, (0, HEAD_PAD - MLA_QK)).reshape(-1, 1)

    h = _ffn(h, row(ffn1_norm), ffn1_w_in.astype(BF16), ffn1_w_out.astype(BF16))
    qt, k, vt, sqt, sk, svt = _mix_proj(h, row(mix_norm), w_lat, w_sb, row(q_latent_norm), w_q, row(kv_latent_norm),
                                        w_kv, gqh_col, gkh_col, rope, tile)
    seq = lambda a: a.reshape(b, s, a.shape[-1])
    blocks = lambda a: a.reshape(b, s // tile, a.shape[1], tile)
    o_mla = _mla_attention(blocks(qt), seq(k), blocks(vt))
    o_sb = _sb_attention(blocks(sqt), seq(sk), blocks(svt))
    h = _merge(h, row(mix_norm), w_gates, o_mla.reshape(b * s, -1), o_sb.reshape(b * s, -1),
               w_branch_mla.astype(BF16), w_branch_sb.astype(BF16), w_out.astype(BF16))
    return _ffn(h, row(ffn2_norm), ffn2_w_in.astype(BF16), ffn2_w_out.astype(BF16),
                ple=(p, row(ple_norm), w_ple_gate.astype(BF16), w_ple_proj.astype(BF16)))


def kernel(x, p, positions, ffn1_norm, ffn1_w_in, ffn1_w_out, mix_norm, w_in, q_latent_norm, w_q_up, kv_latent_norm, w_kv_up, q_head_norm, k_head_norm, w_branch_mla, w_branch_sb, w_out, ffn2_norm, ffn2_w_in, ffn2_w_out, ple_norm, w_ple_gate, w_ple_proj):
    b, s, _ = x.shape
    tile = min(ATTN_TILE, s)
    rope = _rope_tables(positions, tile)
    weights = (ffn1_norm, ffn1_w_in, ffn1_w_out, mix_norm, w_in, q_latent_norm, w_q_up, kv_latent_norm, w_kv_up,
               q_head_norm, k_head_norm, w_branch_mla, w_branch_sb, w_out, ffn2_norm, ffn2_w_in, ffn2_w_out,
               ple_norm, w_ple_gate, w_ple_proj)
    h = x.reshape(b * s, D_MODEL)
    for i in range(p.shape[0]):
        h = _layer(h, p[i].reshape(b * s, -1), rope, tile, b, s, *(w[i] for w in weights))
    return h.reshape(b, s, D_MODEL)
```

```python
import functools
import math

import jax
import jax.numpy as jnp
from jax import lax
from jax.experimental import pallas as pl
from jax.experimental.pallas import tpu as pltpu

D_MODEL = 1024
D_FF = 2816
FFN_RES_WEIGHT = 0.5
NORM_EPS = 1e-6

MLA_HEADS = 8
MLA_NOPE = 64
MLA_ROPE = 32
MLA_QK = MLA_NOPE + MLA_ROPE
MLA_V = 64
Q_LORA = 384
KV_LORA = 256
ROPE_BASE = 10000.0

SB_HEADS = 8
SB_HEAD_DIM = 64
SB_WIDTH = SB_HEADS * SB_HEAD_DIM
MLA_WIDTH = MLA_HEADS * MLA_V
V_HEAD = 64

LANES = 128
HEAD_PAD = LANES
MLA_QK_PAD = MLA_HEADS * HEAD_PAD
HALF_ROPE = MLA_ROPE // 2

VMEM_LIMIT_BYTES = 56 * 1024 * 1024
TOKEN_TILE = 512
ATTN_TILE = 256
FF_CHUNKS = ((0, 1024), (1024, 1024), (2048, 768))
MASKED_SCORE = -1e30
SB_DEAD_LOG2_WEIGHT = 151.0

BF16 = jnp.bfloat16
F32 = jnp.float32


def _dot(a, b):
    return jnp.dot(a, b, preferred_element_type=F32)


def _rms(x, g, n=None):
    n = x.shape[-1] if n is None else n
    ms = jnp.sum(x * x, axis=-1, keepdims=True) * (1.0 / n)
    return x * lax.rsqrt(ms + NORM_EPS) * g


def _sigmoid(x):
    return 0.5 * jnp.tanh(0.5 * x) + 0.5


def _const_spec(shape):
    nd = len(shape)
    return pl.BlockSpec(shape, lambda *_: (0,) * nd, pipeline_mode=pl.Buffered(1))


def _row_spec(tile, width):
    return pl.BlockSpec((tile, width), lambda i: (i, 0))


def _params(n_axes):
    return pltpu.CompilerParams(dimension_semantics=("arbitrary",) * n_axes,
                                vmem_limit_bytes=VMEM_LIMIT_BYTES)


def _ffn_kernel(*refs, with_merge, with_ple):
    refs = list(refs)
    take = lambda n: [refs.pop(0) for _ in range(n)]
    (h_ref,) = take(1)
    h = h_ref[...]
    if with_merge:
        gmix_ref, wg_ref, om_ref, os_ref, wbm_ref, wbs_ref, wo_ref = take(7)
        gates = _sigmoid(_dot(_rms(h, gmix_ref[...]).astype(BF16), wg_ref[...]))
        merged = (gates[:, :D_MODEL] * _dot(om_ref[...], wbm_ref[...])
                  + gates[:, D_MODEL:] * _dot(os_ref[...], wbs_ref[...]))
        h = h + _dot(merged.astype(BF16), wo_ref[...])
    g_ref, win_ref, wout_ref = take(3)
    u = _rms(h, g_ref[...]).astype(BF16)
    acc = None
    for c0, cw in FF_CHUNKS:
        a = _dot(u, win_ref[:, c0:c0 + cw])
        b = _dot(u, win_ref[:, D_FF + c0:D_FF + c0 + cw])
        y = _dot((a * _sigmoid(a) * b).astype(BF16), wout_ref[c0:c0 + cw, :])
        acc = y if acc is None else acc + y
    h = h + FFN_RES_WEIGHT * acc
    if with_ple:
        p_ref, pg_ref, wpg_ref, wpp_ref = take(4)
        gate = _sigmoid(_dot(_rms(h, pg_ref[...]).astype(BF16), wpg_ref[...]))
        h = h + gate * _dot(p_ref[...].astype(BF16), wpp_ref[...])
    (o_ref,) = refs
    o_ref[...] = h


def _ffn(h, g, w_in, w_out, merge=None, ple=None):
    t = h.shape[0]
    tm = min(TOKEN_TILE, t)
    tiled = lambda a: (a, _row_spec(tm, a.shape[1]))
    const = lambda a: (a, _const_spec(a.shape))
    ops = [tiled(h)]
    if merge is not None:
        gmix, wg, om, osb, wbm, wbs, wo = merge
        ops += [const(gmix), const(wg), tiled(om), tiled(osb), const(wbm), const(wbs), const(wo)]
    ops += [const(g), const(w_in), const(w_out)]
    if ple is not None:
        p, pg, wpg, wpp = ple
        ops += [tiled(p), const(pg), const(wpg), const(wpp)]
    return pl.pallas_call(
        functools.partial(_ffn_kernel, with_merge=merge is not None, with_ple=ple is not None),
        out_shape=jax.ShapeDtypeStruct(h.shape, F32),
        grid=(t // tm,),
        in_specs=[spec for _, spec in ops],
        out_specs=_row_spec(tm, D_MODEL),
        compiler_params=_params(1),
        name="merge_ffn_ple" if merge is not None else "ffn",
    )(*[a for a, _ in ops])


def _head_norm_rope_t(xt, g_col, cos_t, sin_t):
    ms = jnp.sum(xt * xt, axis=0, keepdims=True) * (1.0 / MLA_QK)
    xt = xt * lax.rsqrt(ms + NORM_EPS) * g_col
    x1, x2 = xt[MLA_NOPE:MLA_NOPE + HALF_ROPE], xt[MLA_NOPE + HALF_ROPE:MLA_QK]
    return jnp.concatenate([xt[:MLA_NOPE], x1 * cos_t - x2 * sin_t, x2 * cos_t + x1 * sin_t, xt[MLA_QK:]], axis=0)


def _store_blocks_transposed(t_ref, v, tile, col0=0, finish=lambda c, vt: vt):
    for c in range(v.shape[0] // tile):
        t_ref[c, col0:col0 + v.shape[1], :] = finish(c, v[c * tile:(c + 1) * tile, :].T).astype(BF16)


def _rows_via_transposed(v, tile, finish):
    return jnp.concatenate([finish(c, v[c * tile:(c + 1) * tile, :].T).T for c in range(v.shape[0] // tile)], axis=0)


def _mix_proj_kernel(h_ref, gmix_ref, wlat_ref, wsb_ref, gq_ref, wq_ref, gkv_ref, wkv_ref, gqh_ref, gkh_ref,
                     cost_ref, sint_ref, qt_ref, k_ref, vt_ref, sqt_ref, sk_ref, svt_ref, *, tile):
    u = _rms(h_ref[...], gmix_ref[...]).astype(BF16)
    lat = _dot(u, wlat_ref[...])
    sb = _dot(u, wsb_ref[...])
    scale_sb = 1.0 / math.sqrt(SB_HEAD_DIM)
    _store_blocks_transposed(sqt_ref, sb[:, :SB_WIDTH] * scale_sb, tile)
    sk_ref[...] = sb[:, SB_WIDTH:2 * SB_WIDTH].astype(BF16)
    _store_blocks_transposed(svt_ref, sb[:, 2 * SB_WIDTH:], tile)

    c_q = lat[:, :Q_LORA]
    c_kv = lat[:, Q_LORA:Q_LORA + KV_LORA]
    k_rope = lat[:, Q_LORA + KV_LORA:]
    q = _dot(_rms(c_q, gq_ref[...]).astype(BF16), wq_ref[...])
    kv = _dot(_rms(c_kv, gkv_ref[...]).astype(BF16), wkv_ref[...])
    _store_blocks_transposed(vt_ref, kv[:, MLA_QK_PAD:], tile)
    gqh_col, gkh_col = gqh_ref[...], gkh_ref[...]
    q_finish = lambda c, xt: _head_norm_rope_t(xt, gqh_col, cost_ref[c], sint_ref[c])
    k_finish = lambda c, xt: _head_norm_rope_t(xt, gkh_col, cost_ref[c], sint_ref[c])
    for hd in range(MLA_HEADS):
        sl = slice(hd * HEAD_PAD, (hd + 1) * HEAD_PAD)
        _store_blocks_transposed(qt_ref, q[:, sl], tile, hd * HEAD_PAD, q_finish)
        k_ref[:, sl] = _rows_via_transposed(kv[:, sl] + k_rope, tile, k_finish).astype(BF16)


def _mix_proj(h, gmix, wlat, wsb, gq, wq, gkv, wkv, gqh_col, gkh_col, rope, tile):
    t = h.shape[0]
    tm = min(TOKEN_TILE, t)
    cos_t, sin_t = rope
    consts = [gmix, wlat, wsb, gq, wq, gkv, wkv, gqh_col, gkh_col]
    angle_blocks = pl.BlockSpec((tm // tile, HALF_ROPE, tile), lambda i: (i, 0, 0))
    rows = lambda w: (jax.ShapeDtypeStruct((t, w), BF16), _row_spec(tm, w))
    blocks_t = lambda w: (jax.ShapeDtypeStruct((t // tile, w, tile), BF16),
                          pl.BlockSpec((tm // tile, w, tile), lambda i: (i, 0, 0)))
    outs = [blocks_t(MLA_QK_PAD), rows(MLA_QK_PAD), blocks_t(MLA_WIDTH),
            blocks_t(SB_WIDTH), rows(SB_WIDTH), blocks_t(SB_WIDTH)]
    return pl.pallas_call(
        functools.partial(_mix_proj_kernel, tile=tile),
        out_shape=[o[0] for o in outs],
        grid=(t // tm,),
        in_specs=[_row_spec(tm, D_MODEL)] + [_const_spec(c.shape) for c in consts]
                 + [angle_blocks, angle_blocks],
        out_specs=[o[1] for o in outs],
        compiler_params=_params(1),
        name="mix_proj",
    )(h, *consts, cos_t, sin_t)


def _tile_iotas(tile):
    key = lax.broadcasted_iota(jnp.int32, (tile, tile), 0)
    qry = lax.broadcasted_iota(jnp.int32, (tile, tile), 1)
    return key, qry


def _key_rows(j, tile):
    return pl.ds(pl.multiple_of(j * tile, tile), tile)


def _mla_attn_kernel(qt_ref, k_ref, vt_ref, o_ref, s_scr, p_scr, bias_scr, *, tile, n_tiles):
    c = math.log2(math.e) / math.sqrt(MLA_QK)
    heads = [slice(hd * HEAD_PAD, (hd + 1) * HEAD_PAD) for hd in range(2)]
    n_steps = n_tiles * (n_tiles + 1) // 2

    def scores(i, j, slot, may_be_diagonal):
        k_rows = _key_rows(j, tile)
        tops = []
        for hd in range(2):
            s = _dot(k_ref[0, k_rows, heads[hd]], qt_ref[0, i, heads[hd], :])
            if may_be_diagonal:
                s = s + bias_scr[(i == j).astype(jnp.int32)]
            s_scr[slot, hd] = s
            tops.append(jnp.max(s, axis=0, keepdims=True))
        return tuple(tops)

    def weighted_values(j, slot, alphas, accs):
        return tuple(alphas[hd] * accs[hd] + _dot(vt_ref[0, j, hd * V_HEAD:(hd + 1) * V_HEAD, :], p_scr[slot, hd])
                     for hd in range(2))

    def step(slot, carry, may_be_diagonal=True):
        i, j, tops, ms, ls, alphas, accs = carry
        on_diag = j == i
        next_i, next_j = jnp.where(on_diag, i + 1, i), jnp.where(on_diag, 0, j + 1)
        next_tops = scores(jnp.minimum(next_i, n_tiles - 1), next_j, 1 - slot, may_be_diagonal)
        prev_j = jnp.maximum(jnp.where(j > 0, j - 1, i - 1), 0)
        accs = weighted_values(prev_j, 1 - slot, alphas, accs)
        new_tile = jnp.logical_and(j == 0, i > 0)
        done = (new_tile, jnp.maximum(i - 1, 0), ls, accs)
        keep = jnp.where(new_tile, 0.0, 1.0)
        ms = tuple(m * keep + MASKED_SCORE * (1.0 - keep) for m in ms)
        out = []
        for hd in range(2):
            m_new = jnp.maximum(ms[hd], tops[hd])
            alpha = jnp.exp2((ms[hd] - m_new) * c)
            p = jnp.exp2((s_scr[slot, hd] - m_new) * c)
            p_scr[slot, hd] = p.astype(BF16)
            out.append((m_new, alpha * ls[hd] + jnp.sum(p, axis=0, keepdims=True), alpha))
        ms, ls, alphas = zip(*out)
        return (next_i, next_j, next_tops, ms, ls, alphas, accs), done

    def store_tile(i, ls, accs):
        o_t = jnp.concatenate([accs[hd] / ls[hd] for hd in range(2)], axis=0)
        o_ref[0, _key_rows(i, tile), :] = o_t.T.astype(BF16)

    def store_if_done(done):
        flag, i, ls, accs = done
        pl.when(flag)(lambda: store_tile(i, ls, accs))

    key, qry = _tile_iotas(tile)
    bias_scr[0] = jnp.zeros((tile, tile), F32)
    bias_scr[1] = jnp.where(key <= qry, 0.0, MASKED_SCORE)
    p_scr[1] = jnp.zeros(p_scr.shape[1:], BF16)
    zero = jnp.int32(0)
    tops = scores(zero, zero, 0, True)
    row = lambda v: (jnp.full((1, tile), v, F32),) * 2
    carry = (zero, zero, tops, row(MASKED_SCORE), row(0.0), row(1.0), (jnp.zeros((V_HEAD, tile), F32),) * 2)
    slots = (0, 1)
    if n_steps % 2:
        carry, done = step(0, carry)
        store_if_done(done)
        slots = (1, 0)

    def two_steps(carry, may_be_diagonal):
        carry, done_a = step(slots[0], carry, may_be_diagonal)
        carry, done_b = step(slots[1], carry, may_be_diagonal)
        store_if_done(done_a)
        store_if_done(done_b)
        return carry

    def trip(_, carry):
        i, j = carry[0], carry[1]
        gap = i - j
        diagonal_ahead = jnp.logical_or(jnp.logical_or(gap == 1, gap == 2), i == 0)
        return lax.cond(diagonal_ahead, functools.partial(two_steps, may_be_diagonal=True),
                        functools.partial(two_steps, may_be_diagonal=False), carry)

    _, _, _, _, ls, alphas, accs = lax.fori_loop(0, n_steps // 2, trip, carry)
    accs = weighted_values(n_tiles - 1, (n_steps - 1) % 2, alphas, accs)
    store_tile(n_tiles - 1, ls, accs)


def _sb_attn_kernel(qt_ref, k_ref, vt_ref, o_ref, *, tile, n_tiles):
    log2e = math.log2(math.e)
    key, qry = _tile_iotas(tile)
    strict = key < qry
    from_key = (qry >= key).astype(BF16)
    low = lax.broadcasted_iota(jnp.int32, (LANES, tile), 0) < V_HEAD

    def query_tile(i, _):
        q_pair = qt_ref[0, i]
        qs = [jnp.where(low, q_pair, jnp.zeros_like(q_pair)), jnp.where(low, jnp.zeros_like(q_pair), q_pair)]

        def two_blocks(j, deads, accs, masked):
            has_second = j >= 1
            idx = (j, jnp.maximum(j - 1, 0))
            k_blks = [k_ref[0, _key_rows(jb, tile), :] for jb in idx]
            z2s = [[_dot(k_blks[b], qs[hd]) * log2e for hd in range(2)] for b in range(2)]
            incl = [[None, None], [None, None]]
            for b in range(2):
                for hd in range(2):
                    z2 = z2s[b][hd]
                    x = jnp.maximum(z2, 0.0) + jnp.log2(1.0 + jnp.exp2(-jnp.abs(z2)))
                    if masked and b == 0:
                        x = jnp.where(strict, x, 0.0)
                    hi = x.astype(BF16)
                    lo = (x - hi.astype(F32)).astype(BF16)
                    incl[b][hd] = _dot(from_key, hi) + _dot(from_key, lo)
            new_deads, new_accs = [], []
            for hd in range(2):
                dead = [deads[hd], jnp.where(has_second, deads[hd] + incl[0][hd][0:1, :], jnp.inf)]
                acc = accs[hd]
                for b in range(2):
                    a = jnp.exp2(z2s[b][hd] - incl[b][hd] - dead[b])
                    if masked and b == 0:
                        a = jnp.where(strict, a, 0.0)
                    acc = acc + _dot(vt_ref[0, idx[b], hd * V_HEAD:(hd + 1) * V_HEAD, :], a.astype(BF16))
                new_deads.append(dead[1] + incl[1][hd][0:1, :])
                new_accs.append(acc)
            return tuple(new_deads), tuple(new_accs)

        def alive(deads):
            return (jnp.minimum(jnp.min(deads[0]), jnp.min(deads[1])) <= SB_DEAD_LOG2_WEIGHT).astype(jnp.int32)

        deads, accs = two_blocks(i, (jnp.zeros((1, tile), F32),) * 2, (jnp.zeros((V_HEAD, tile), F32),) * 2, True)

        def more(carry):
            j, live, _, _ = carry
            return jnp.logical_and(j >= 0, live > 0)

        def body(carry):
            j, _, deads, accs = carry
            deads, accs = two_blocks(j, deads, accs, False)
            return j - 2, alive(deads), deads, accs

        _, _, _, accs = lax.while_loop(more, body, (i - 2, alive(deads), deads, accs))
        o_t = jnp.concatenate(accs, axis=0)
        o_ref[0, _key_rows(i, tile), :] = o_t.T.astype(BF16)
        return 0

    lax.fori_loop(0, n_tiles, query_tile, 0)


def _blocks_t_spec(n_tiles, rows, tile):
    return pl.BlockSpec((1, n_tiles, rows, tile), lambda b, p: (b, 0, p, 0))


def _mla_attention(qt, k, vt):
    b, n_tiles, _, tile = qt.shape
    s = n_tiles * tile
    return pl.pallas_call(
        functools.partial(_mla_attn_kernel, tile=tile, n_tiles=n_tiles),
        out_shape=jax.ShapeDtypeStruct((b, s, MLA_WIDTH), BF16),
        grid=(b, MLA_HEADS // 2),
        in_specs=[_blocks_t_spec(n_tiles, 2 * HEAD_PAD, tile),
                  pl.BlockSpec((1, s, 2 * HEAD_PAD), lambda b, p: (b, 0, p)),
                  _blocks_t_spec(n_tiles, LANES, tile)],
        out_specs=pl.BlockSpec((1, s, LANES), lambda b, p: (b, 0, p)),
        scratch_shapes=[pltpu.VMEM((2, 2, tile, tile), F32), pltpu.VMEM((2, 2, tile, tile), BF16),
                        pltpu.VMEM((2, tile, tile), F32)],
        compiler_params=_params(2),
        name="mla_attn",
    )(qt, k, vt)


def _sb_attention(qt, k, vt):
    b, n_tiles, _, tile = qt.shape
    s = n_tiles * tile
    seq_pair = pl.BlockSpec((1, s, LANES), lambda b, p: (b, 0, p))
    return pl.pallas_call(
        functools.partial(_sb_attn_kernel, tile=tile, n_tiles=n_tiles),
        out_shape=jax.ShapeDtypeStruct((b, s, SB_WIDTH), BF16),
        grid=(b, SB_HEADS // 2),
        in_specs=[_blocks_t_spec(n_tiles, LANES, tile), seq_pair, _blocks_t_spec(n_tiles, LANES, tile)],
        out_specs=seq_pair,
        compiler_params=_params(2),
        name="sb_attn",
    )(qt, k, vt)


def _pad_heads(w, head_dim, real):
    lead = w.shape[:-1]
    w = w.reshape(lead + (-1, head_dim))[..., :real]
    w = jnp.pad(w, [(0, 0)] * len(lead) + [(0, 0), (0, HEAD_PAD - real)])
    return w.reshape(lead + (-1,))


def _rope_tables(positions, tile):
    inv_freq = ROPE_BASE ** (-jnp.arange(0, MLA_ROPE, 2, dtype=F32) / MLA_ROPE)
    ang_t = inv_freq.reshape(-1, 1) * positions.astype(F32).reshape(1, -1)
    blocks_t = lambda a: a.reshape(HALF_ROPE, -1, tile).transpose(1, 0, 2)
    return blocks_t(jnp.cos(ang_t)), blocks_t(jnp.sin(ang_t))


def _layer(h, p, rope, tile, b, s, ffn1_norm, ffn1_w_in, ffn1_w_out, mix_norm, w_in, q_latent_norm, w_q_up,
           kv_latent_norm, w_kv_up, q_head_norm, k_head_norm, w_branch_mla, w_branch_sb, w_out, ffn2_norm,
           ffn2_w_in, ffn2_w_out, ple_norm, w_ple_gate, w_ple_proj):
    row = lambda g: g.reshape(1, -1)
    c0, c1, c2, c3 = Q_LORA, Q_LORA + KV_LORA, Q_LORA + KV_LORA + MLA_ROPE, Q_LORA + KV_LORA + MLA_ROPE + 3 * SB_WIDTH
    w_krope = jnp.pad(w_in[:, c1:c2], ((0, 0), (MLA_NOPE, HEAD_PAD - MLA_QK)))
    w_lat = jnp.concatenate([w_in[:, :c1], w_krope], axis=1).astype(BF16)
    w_sb = w_in[:, c2:c3].astype(BF16)
    w_gates = w_in[:, c3:].astype(BF16)
    w_q = _pad_heads(w_q_up, MLA_QK, MLA_QK).astype(BF16)
    kv_heads = w_kv_up.reshape(KV_LORA, MLA_HEADS, MLA_NOPE + MLA_V)
    w_knope = _pad_heads(kv_heads[..., :MLA_NOPE].reshape(KV_LORA, -1), MLA_NOPE, MLA_NOPE)
    w_kv = jnp.concatenate([w_knope, kv_heads[..., MLA_NOPE:].reshape(KV_LORA, -1)], axis=1).astype(BF16)
    gqh_col = jnp.pad(q_head_norm, (0, HEAD_PAD - MLA_QK)).reshape(-1, 1)
    gkh_col = jnp.pad(k_head_norm, (0, HEAD_PAD - MLA_QK)).reshape(-1, 1)

    h = _ffn(h, row(ffn1_norm), ffn1_w_in.astype(BF16), ffn1_w_out.astype(BF16))
    qt, k, vt, sqt, sk, svt = _mix_proj(h, row(mix_norm), w_lat, w_sb, row(q_latent_norm), w_q, row(kv_latent_norm),
                                        w_kv, gqh_col, gkh_col, rope, tile)
    seq = lambda a: a.reshape(b, s, a.shape[-1])
    blocks = lambda a: a.reshape(b, s // tile, a.shape[1], tile)
    o_mla = _mla_attention(blocks(qt), seq(k), blocks(vt))
    o_sb = _sb_attention(blocks(sqt), seq(sk), blocks(svt))
    return _ffn(h, row(ffn2_norm), ffn2_w_in.astype(BF16), ffn2_w_out.astype(BF16),
                merge=(row(mix_norm), w_gates, o_mla.reshape(b * s, -1), o_sb.reshape(b * s, -1),
                       w_branch_mla.astype(BF16), w_branch_sb.astype(BF16), w_out.astype(BF16)),
                ple=(p, row(ple_norm), w_ple_gate.astype(BF16), w_ple_proj.astype(BF16)))


def kernel(x, p, positions, ffn1_norm, ffn1_w_in, ffn1_w_out, mix_norm, w_in, q_latent_norm, w_q_up, kv_latent_norm, w_kv_up, q_head_norm, k_head_norm, w_branch_mla, w_branch_sb, w_out, ffn2_norm, ffn2_w_in, ffn2_w_out, ple_norm, w_ple_gate, w_ple_proj):
    b, s, _ = x.shape
    tile = min(ATTN_TILE, s)
    rope = _rope_tables(positions, tile)
    weights = (ffn1_norm, ffn1_w_in, ffn1_w_out, mix_norm, w_in, q_latent_norm, w_q_up, kv_latent_norm, w_kv_up,
               q_head_norm, k_head_norm, w_branch_mla, w_branch_sb, w_out, ffn2_norm, ffn2_w_in, ffn2_w_out,
               ple_norm, w_ple_gate, w_ple_proj)
    h = x.reshape(b * s, D_MODEL)
    for i in range(p.shape[0]):
        h = _layer(h, p[i].reshape(b * s, -1), rope, tile, b, s, *(w[i] for w in weights))
    return h.reshape(b, s, D_MODEL)
```

```python
import functools
import math

import jax
import jax.numpy as jnp
from jax import lax
from jax.experimental import pallas as pl
from jax.experimental.pallas import tpu as pltpu

D_MODEL = 1024
D_FF = 2816
FFN_RES_WEIGHT = 0.5
NORM_EPS = 1e-6

MLA_HEADS = 8
MLA_NOPE = 64
MLA_ROPE = 32
MLA_QK = MLA_NOPE + MLA_ROPE
MLA_V = 64
Q_LORA = 384
KV_LORA = 256
ROPE_BASE = 10000.0

SB_HEADS = 8
SB_HEAD_DIM = 64
SB_WIDTH = SB_HEADS * SB_HEAD_DIM
MLA_WIDTH = MLA_HEADS * MLA_V
V_HEAD = 64

LANES = 128
HEAD_PAD = LANES
MLA_QK_PAD = MLA_HEADS * HEAD_PAD
HALF_ROPE = MLA_ROPE // 2

VMEM_LIMIT_BYTES = 56 * 1024 * 1024
TOKEN_TILE = 512
MLA_TILE = 512
SB_TILE = 256
FF_CHUNKS = ((0, 1024), (1024, 1024), (2048, 768))
MASKED_SCORE = -1e30
SB_DEAD_LOG2_WEIGHT = 151.0

BF16 = jnp.bfloat16
F32 = jnp.float32


def _dot(a, b):
    return jnp.dot(a, b, preferred_element_type=F32)


def _rms(x, g, n=None):
    n = x.shape[-1] if n is None else n
    ms = jnp.sum(x * x, axis=-1, keepdims=True) * (1.0 / n)
    return x * lax.rsqrt(ms + NORM_EPS) * g


def _sigmoid(x):
    return 0.5 * jnp.tanh(0.5 * x) + 0.5


def _const_spec(shape):
    nd = len(shape)
    return pl.BlockSpec(shape, lambda *_: (0,) * nd, pipeline_mode=pl.Buffered(1))


def _row_spec(tile, width):
    return pl.BlockSpec((tile, width), lambda i: (i, 0))


def _params(n_axes):
    return pltpu.CompilerParams(dimension_semantics=("arbitrary",) * n_axes,
                                vmem_limit_bytes=VMEM_LIMIT_BYTES)


def _ffn_kernel(*refs, with_merge, with_ple):
    refs = list(refs)
    take = lambda n: [refs.pop(0) for _ in range(n)]
    (h_ref,) = take(1)
    h = h_ref[...]
    if with_merge:
        gmix_ref, wg_ref, om_ref, os_ref, wbm_ref, wbs_ref, wo_ref = take(7)
        gates = _sigmoid(_dot(_rms(h, gmix_ref[...]).astype(BF16), wg_ref[...]))
        merged = (gates[:, :D_MODEL] * _dot(om_ref[...], wbm_ref[...])
                  + gates[:, D_MODEL:] * _dot(os_ref[...], wbs_ref[...]))
        h = h + _dot(merged.astype(BF16), wo_ref[...])
    g_ref, win_ref, wout_ref = take(3)
    u = _rms(h, g_ref[...]).astype(BF16)
    acc = None
    for c0, cw in FF_CHUNKS:
        a = _dot(u, win_ref[:, c0:c0 + cw])
        b = _dot(u, win_ref[:, D_FF + c0:D_FF + c0 + cw])
        y = _dot((a * _sigmoid(a) * b).astype(BF16), wout_ref[c0:c0 + cw, :])
        acc = y if acc is None else acc + y
    h = h + FFN_RES_WEIGHT * acc
    if with_ple:
        p_ref, pg_ref, wpg_ref, wpp_ref = take(4)
        gate = _sigmoid(_dot(_rms(h, pg_ref[...]).astype(BF16), wpg_ref[...]))
        h = h + gate * _dot(p_ref[...].astype(BF16), wpp_ref[...])
    (o_ref,) = refs
    o_ref[...] = h


def _ffn(h, g, w_in, w_out, merge=None, ple=None):
    t = h.shape[0]
    tm = min(TOKEN_TILE, t)
    tiled = lambda a: (a, _row_spec(tm, a.shape[1]))
    const = lambda a: (a, _const_spec(a.shape))
    ops = [tiled(h)]
    if merge is not None:
        gmix, wg, om, osb, wbm, wbs, wo = merge
        ops += [const(gmix), const(wg), tiled(om), tiled(osb), const(wbm), const(wbs), const(wo)]
    ops += [const(g), const(w_in), const(w_out)]
    if ple is not None:
        p, pg, wpg, wpp = ple
        ops += [tiled(p), const(pg), const(wpg), const(wpp)]
    return pl.pallas_call(
        functools.partial(_ffn_kernel, with_merge=merge is not None, with_ple=ple is not None),
        out_shape=jax.ShapeDtypeStruct(h.shape, F32),
        grid=(t // tm,),
        in_specs=[spec for _, spec in ops],
        out_specs=_row_spec(tm, D_MODEL),
        compiler_params=_params(1),
        name="merge_ffn_ple" if merge is not None else "ffn",
    )(*[a for a, _ in ops])


def _head_norm_rope_t(xt, g_col, cos_t, sin_t):
    ms = jnp.sum(xt * xt, axis=0, keepdims=True) * (1.0 / MLA_QK)
    xt = xt * lax.rsqrt(ms + NORM_EPS) * g_col
    x1, x2 = xt[MLA_NOPE:MLA_NOPE + HALF_ROPE], xt[MLA_NOPE + HALF_ROPE:MLA_QK]
    return jnp.concatenate([xt[:MLA_NOPE], x1 * cos_t - x2 * sin_t, x2 * cos_t + x1 * sin_t, xt[MLA_QK:]], axis=0)


def _store_blocks_transposed(t_ref, v, tile, col0=0, finish=lambda c, vt: vt):
    for c in range(v.shape[0] // tile):
        t_ref[c, col0:col0 + v.shape[1], :] = finish(c, v[c * tile:(c + 1) * tile, :].T).astype(BF16)


def _rows_via_transposed(v, tile, finish):
    return jnp.concatenate([finish(c, v[c * tile:(c + 1) * tile, :].T).T for c in range(v.shape[0] // tile)], axis=0)


def _mix_proj_kernel(h_ref, gmix_ref, wlat_ref, wsb_ref, gq_ref, wq_ref, gkv_ref, wkv_ref, gqh_ref, gkh_ref,
                     cost_ref, sint_ref, qt_ref, k_ref, vt_ref, sqt_ref, sk_ref, svt_ref, *, tile, sb_tile):
    u = _rms(h_ref[...], gmix_ref[...]).astype(BF16)
    lat = _dot(u, wlat_ref[...])
    sb = _dot(u, wsb_ref[...])
    scale_sb = 1.0 / math.sqrt(SB_HEAD_DIM)
    _store_blocks_transposed(sqt_ref, sb[:, :SB_WIDTH] * scale_sb, sb_tile)
    sk_ref[...] = sb[:, SB_WIDTH:2 * SB_WIDTH].astype(BF16)
    _store_blocks_transposed(svt_ref, sb[:, 2 * SB_WIDTH:], sb_tile)

    c_q = lat[:, :Q_LORA]
    c_kv = lat[:, Q_LORA:Q_LORA + KV_LORA]
    k_rope = lat[:, Q_LORA + KV_LORA:]
    q = _dot(_rms(c_q, gq_ref[...]).astype(BF16), wq_ref[...])
    kv = _dot(_rms(c_kv, gkv_ref[...]).astype(BF16), wkv_ref[...])
    _store_blocks_transposed(vt_ref, kv[:, MLA_QK_PAD:], tile)
    gqh_col, gkh_col = gqh_ref[...], gkh_ref[...]
    q_finish = lambda c, xt: _head_norm_rope_t(xt, gqh_col, cost_ref[c], sint_ref[c])
    k_finish = lambda c, xt: _head_norm_rope_t(xt, gkh_col, cost_ref[c], sint_ref[c])
    for hd in range(MLA_HEADS):
        sl = slice(hd * HEAD_PAD, (hd + 1) * HEAD_PAD)
        _store_blocks_transposed(qt_ref, q[:, sl], tile, hd * HEAD_PAD, q_finish)
        k_ref[:, sl] = _rows_via_transposed(kv[:, sl] + k_rope, tile, k_finish).astype(BF16)


def _mix_proj(h, gmix, wlat, wsb, gq, wq, gkv, wkv, gqh_col, gkh_col, rope, tile, sb_tile):
    t = h.shape[0]
    tm = min(TOKEN_TILE, t)
    cos_t, sin_t = rope
    consts = [gmix, wlat, wsb, gq, wq, gkv, wkv, gqh_col, gkh_col]
    angle_blocks = pl.BlockSpec((tm // tile, HALF_ROPE, tile), lambda i: (i, 0, 0))
    rows = lambda w: (jax.ShapeDtypeStruct((t, w), BF16), _row_spec(tm, w))
    blocks_t = lambda w, tl: (jax.ShapeDtypeStruct((t // tl, w, tl), BF16),
                              pl.BlockSpec((tm // tl, w, tl), lambda i: (i, 0, 0)))
    outs = [blocks_t(MLA_QK_PAD, tile), rows(MLA_QK_PAD), blocks_t(MLA_WIDTH, tile),
            blocks_t(SB_WIDTH, sb_tile), rows(SB_WIDTH), blocks_t(SB_WIDTH, sb_tile)]
    return pl.pallas_call(
        functools.partial(_mix_proj_kernel, tile=tile, sb_tile=sb_tile),
        out_shape=[o[0] for o in outs],
        grid=(t // tm,),
        in_specs=[_row_spec(tm, D_MODEL)] + [_const_spec(c.shape) for c in consts]
                 + [angle_blocks, angle_blocks],
        out_specs=[o[1] for o in outs],
        compiler_params=_params(1),
        name="mix_proj",
    )(h, *consts, cos_t, sin_t)


def _tile_iotas(tile):
    key = lax.broadcasted_iota(jnp.int32, (tile, tile), 0)
    qry = lax.broadcasted_iota(jnp.int32, (tile, tile), 1)
    return key, qry


def _key_rows(j, tile):
    return pl.ds(pl.multiple_of(j * tile, tile), tile)


def _mla_attn_kernel(qt_ref, k_ref, vt_ref, o_ref, s_scr, p_scr, bias_scr, *, tile, n_tiles):
    c = math.log2(math.e) / math.sqrt(MLA_QK)
    heads = [slice(hd * HEAD_PAD, (hd + 1) * HEAD_PAD) for hd in range(2)]
    n_steps = n_tiles * (n_tiles + 1) // 2

    def scores(i, j, slot, may_be_diagonal):
        k_rows = _key_rows(j, tile)
        tops = []
        for hd in range(2):
            s = _dot(k_ref[0, k_rows, heads[hd]], qt_ref[0, i, heads[hd], :])
            if may_be_diagonal:
                s = s + bias_scr[(i == j).astype(jnp.int32)]
            s_scr[slot, hd] = s
            tops.append(jnp.max(s, axis=0, keepdims=True))
        return tuple(tops)

    def weighted_values(j, slot, alphas, accs):
        return tuple(alphas[hd] * accs[hd] + _dot(vt_ref[0, j, hd * V_HEAD:(hd + 1) * V_HEAD, :], p_scr[slot, hd])
                     for hd in range(2))

    def step(slot, carry, may_be_diagonal=True):
        i, j, tops, ms, ls, alphas, accs = carry
        on_diag = j == i
        next_i, next_j = jnp.where(on_diag, i + 1, i), jnp.where(on_diag, 0, j + 1)
        next_tops = scores(jnp.minimum(next_i, n_tiles - 1), next_j, 1 - slot, may_be_diagonal)
        prev_j = jnp.maximum(jnp.where(j > 0, j - 1, i - 1), 0)
        accs = weighted_values(prev_j, 1 - slot, alphas, accs)
        new_tile = jnp.logical_and(j == 0, i > 0)
        done = (new_tile, jnp.maximum(i - 1, 0), ls, accs)
        keep = jnp.where(new_tile, 0.0, 1.0)
        ms = tuple(m * keep + MASKED_SCORE * (1.0 - keep) for m in ms)
        out = []
        for hd in range(2):
            m_new = jnp.maximum(ms[hd], tops[hd])
            alpha = jnp.exp2((ms[hd] - m_new) * c)
            p = jnp.exp2((s_scr[slot, hd] - m_new) * c)
            p_scr[slot, hd] = p.astype(BF16)
            out.append((m_new, alpha * ls[hd] + jnp.sum(p, axis=0, keepdims=True), alpha))
        ms, ls, alphas = zip(*out)
        return (next_i, next_j, next_tops, ms, ls, alphas, accs), done

    def store_tile(i, ls, accs):
        o_t = jnp.concatenate([accs[hd] / ls[hd] for hd in range(2)], axis=0)
        o_ref[0, _key_rows(i, tile), :] = o_t.T.astype(BF16)

    def store_if_done(done):
        flag, i, ls, accs = done
        pl.when(flag)(lambda: store_tile(i, ls, accs))

    key, qry = _tile_iotas(tile)
    bias_scr[0] = jnp.zeros((tile, tile), F32)
    bias_scr[1] = jnp.where(key <= qry, 0.0, MASKED_SCORE)
    p_scr[1] = jnp.zeros(p_scr.shape[1:], BF16)
    zero = jnp.int32(0)
    tops = scores(zero, zero, 0, True)
    row = lambda v: (jnp.full((1, tile), v, F32),) * 2
    carry = (zero, zero, tops, row(MASKED_SCORE), row(0.0), row(1.0), (jnp.zeros((V_HEAD, tile), F32),) * 2)
    slots = (0, 1)
    if n_steps % 2:
        carry, done = step(0, carry)
        store_if_done(done)
        slots = (1, 0)

    def two_steps(carry, may_be_diagonal):
        carry, done_a = step(slots[0], carry, may_be_diagonal)
        carry, done_b = step(slots[1], carry, may_be_diagonal)
        store_if_done(done_a)
        store_if_done(done_b)
        return carry

    def trip(_, carry):
        i, j = carry[0], carry[1]
        gap = i - j
        diagonal_ahead = jnp.logical_or(jnp.logical_or(gap == 1, gap == 2), i == 0)
        return lax.cond(diagonal_ahead, functools.partial(two_steps, may_be_diagonal=True),
                        functools.partial(two_steps, may_be_diagonal=False), carry)

    _, _, _, _, ls, alphas, accs = lax.fori_loop(0, n_steps // 2, trip, carry)
    accs = weighted_values(n_tiles - 1, (n_steps - 1) % 2, alphas, accs)
    store_tile(n_tiles - 1, ls, accs)


def _sb_attn_kernel(qt_ref, k_ref, vt_ref, o_ref, *, tile, n_tiles):
    log2e = math.log2(math.e)
    key, qry = _tile_iotas(tile)
    strict = key < qry
    from_key = (qry >= key).astype(BF16)
    low = lax.broadcasted_iota(jnp.int32, (LANES, tile), 0) < V_HEAD

    def query_tile(i, _):
        q_pair = qt_ref[0, i]
        qs = [jnp.where(low, q_pair, jnp.zeros_like(q_pair)), jnp.where(low, jnp.zeros_like(q_pair), q_pair)]

        def two_blocks(j, deads, accs, masked):
            has_second = j >= 1
            idx = (j, jnp.maximum(j - 1, 0))
            k_blks = [k_ref[0, _key_rows(jb, tile), :] for jb in idx]
            z2s = [[_dot(k_blks[b], qs[hd]) * log2e for hd in range(2)] for b in range(2)]
            incl = [[None, None], [None, None]]
            for b in range(2):
                for hd in range(2):
                    z2 = z2s[b][hd]
                    x = jnp.maximum(z2, 0.0) + jnp.log2(1.0 + jnp.exp2(-jnp.abs(z2)))
                    if masked and b == 0:
                        x = jnp.where(strict, x, 0.0)
                    hi = x.astype(BF16)
                    lo = (x - hi.astype(F32)).astype(BF16)
                    incl[b][hd] = _dot(from_key, hi) + _dot(from_key, lo)
            new_deads, new_accs = [], []
            for hd in range(2):
                dead = [deads[hd], jnp.where(has_second, deads[hd] + incl[0][hd][0:1, :], jnp.inf)]
                acc = accs[hd]
                for b in range(2):
                    a = jnp.exp2(z2s[b][hd] - incl[b][hd] - dead[b])
                    if masked and b == 0:
                        a = jnp.where(strict, a, 0.0)
                    acc = acc + _dot(vt_ref[0, idx[b], hd * V_HEAD:(hd + 1) * V_HEAD, :], a.astype(BF16))
                new_deads.append(dead[1] + incl[1][hd][0:1, :])
                new_accs.append(acc)
            return tuple(new_deads), tuple(new_accs)

        def alive(deads):
            return (jnp.minimum(jnp.min(deads[0]), jnp.min(deads[1])) <= SB_DEAD_LOG2_WEIGHT).astype(jnp.int32)

        deads, accs = two_blocks(i, (jnp.zeros((1, tile), F32),) * 2, (jnp.zeros((V_HEAD, tile), F32),) * 2, True)

        def more(carry):
            j, live, _, _ = carry
            return jnp.logical_and(j >= 0, live > 0)

        def body(carry):
            j, _, deads, accs = carry
            deads, accs = two_blocks(j, deads, accs, False)
            return j - 2, alive(deads), deads, accs

        _, _, _, accs = lax.while_loop(more, body, (i - 2, alive(deads), deads, accs))
        o_t = jnp.concatenate(accs, axis=0)
        o_ref[0, _key_rows(i, tile), :] = o_t.T.astype(BF16)
        return 0

    lax.fori_loop(0, n_tiles, query_tile, 0)


def _blocks_t_spec(n_tiles, rows, tile):
    return pl.BlockSpec((1, n_tiles, rows, tile), lambda b, p: (b, 0, p, 0))


def _mla_attention(qt, k, vt):
    b, n_tiles, _, tile = qt.shape
    s = n_tiles * tile
    return pl.pallas_call(
        functools.partial(_mla_attn_kernel, tile=tile, n_tiles=n_tiles),
        out_shape=jax.ShapeDtypeStruct((b, s, MLA_WIDTH), BF16),
        grid=(b, MLA_HEADS // 2),
        in_specs=[_blocks_t_spec(n_tiles, 2 * HEAD_PAD, tile),
                  pl.BlockSpec((1, s, 2 * HEAD_PAD), lambda b, p: (b, 0, p)),
                  _blocks_t_spec(n_tiles, LANES, tile)],
        out_specs=pl.BlockSpec((1, s, LANES), lambda b, p: (b, 0, p)),
        scratch_shapes=[pltpu.VMEM((2, 2, tile, tile), F32), pltpu.VMEM((2, 2, tile, tile), BF16),
                        pltpu.VMEM((2, tile, tile), F32)],
        compiler_params=_params(2),
        name="mla_attn",
    )(qt, k, vt)


def _sb_attention(qt, k, vt):
    b, n_tiles, _, tile = qt.shape
    s = n_tiles * tile
    seq_pair = pl.BlockSpec((1, s, LANES), lambda b, p: (b, 0, p))
    return pl.pallas_call(
        functools.partial(_sb_attn_kernel, tile=tile, n_tiles=n_tiles),
        out_shape=jax.ShapeDtypeStruct((b, s, SB_WIDTH), BF16),
        grid=(b, SB_HEADS // 2),
        in_specs=[_blocks_t_spec(n_tiles, LANES, tile), seq_pair, _blocks_t_spec(n_tiles, LANES, tile)],
        out_specs=seq_pair,
        compiler_params=_params(2),
        name="sb_attn",
    )(qt, k, vt)


def _pad_heads(w, head_dim, real):
    lead = w.shape[:-1]
    w = w.reshape(lead + (-1, head_dim))[..., :real]
    w = jnp.pad(w, [(0, 0)] * len(lead) + [(0, 0), (0, HEAD_PAD - real)])
    return w.reshape(lead + (-1,))


def _rope_tables(positions, tile):
    inv_freq = ROPE_BASE ** (-jnp.arange(0, MLA_ROPE, 2, dtype=F32) / MLA_ROPE)
    ang_t = inv_freq.reshape(-1, 1) * positions.astype(F32).reshape(1, -1)
    blocks_t = lambda a: a.reshape(HALF_ROPE, -1, tile).transpose(1, 0, 2)
    return blocks_t(jnp.cos(ang_t)), blocks_t(jnp.sin(ang_t))


def _layer(h, p, rope, tile, sb_tile, b, s, ffn1_norm, ffn1_w_in, ffn1_w_out, mix_norm, w_in, q_latent_norm, w_q_up,
           kv_latent_norm, w_kv_up, q_head_norm, k_head_norm, w_branch_mla, w_branch_sb, w_out, ffn2_norm,
           ffn2_w_in, ffn2_w_out, ple_norm, w_ple_gate, w_ple_proj):
    row = lambda g: g.reshape(1, -1)
    c0, c1, c2, c3 = Q_LORA, Q_LORA + KV_LORA, Q_LORA + KV_LORA + MLA_ROPE, Q_LORA + KV_LORA + MLA_ROPE + 3 * SB_WIDTH
    w_krope = jnp.pad(w_in[:, c1:c2], ((0, 0), (MLA_NOPE, HEAD_PAD - MLA_QK)))
    w_lat = jnp.concatenate([w_in[:, :c1], w_krope], axis=1).astype(BF16)
    w_sb = w_in[:, c2:c3].astype(BF16)
    w_gates = w_in[:, c3:].astype(BF16)
    w_q = _pad_heads(w_q_up, MLA_QK, MLA_QK).astype(BF16)
    kv_heads = w_kv_up.reshape(KV_LORA, MLA_HEADS, MLA_NOPE + MLA_V)
    w_knope = _pad_heads(kv_heads[..., :MLA_NOPE].reshape(KV_LORA, -1), MLA_NOPE, MLA_NOPE)
    w_kv = jnp.concatenate([w_knope, kv_heads[..., MLA_NOPE:].reshape(KV_LORA, -1)], axis=1).astype(BF16)
    gqh_col = jnp.pad(q_head_norm, (0, HEAD_PAD - MLA_QK)).reshape(-1, 1)
    gkh_col = jnp.pad(k_head_norm, (0, HEAD_PAD - MLA_QK)).reshape(-1, 1)

    h = _ffn(h, row(ffn1_norm), ffn1_w_in.astype(BF16), ffn1_w_out.astype(BF16))
    qt, k, vt, sqt, sk, svt = _mix_proj(h, row(mix_norm), w_lat, w_sb, row(q_latent_norm), w_q, row(kv_latent_norm),
                                        w_kv, gqh_col, gkh_col, rope, tile, sb_tile)
    seq = lambda a: a.reshape(b, s, a.shape[-1])
    blocks = lambda a: a.reshape(b, -1, *a.shape[1:])
    o_mla = _mla_attention(blocks(qt), seq(k), blocks(vt))
    o_sb = _sb_attention(blocks(sqt), seq(sk), blocks(svt))
    return _ffn(h, row(ffn2_norm), ffn2_w_in.astype(BF16), ffn2_w_out.astype(BF16),
                merge=(row(mix_norm), w_gates, o_mla.reshape(b * s, -1), o_sb.reshape(b * s, -1),
                       w_branch_mla.astype(BF16), w_branch_sb.astype(BF16), w_out.astype(BF16)),
                ple=(p, row(ple_norm), w_ple_gate.astype(BF16), w_ple_proj.astype(BF16)))


def kernel(x, p, positions, ffn1_norm, ffn1_w_in, ffn1_w_out, mix_norm, w_in, q_latent_norm, w_q_up, kv_latent_norm, w_kv_up, q_head_norm, k_head_norm, w_branch_mla, w_branch_sb, w_out, ffn2_norm, ffn2_w_in, ffn2_w_out, ple_norm, w_ple_gate, w_ple_proj):
    b, s, _ = x.shape
    tile, sb_tile = min(MLA_TILE, s), min(SB_TILE, s)
    rope = _rope_tables(positions, tile)
    weights = (ffn1_norm, ffn1_w_in, ffn1_w_out, mix_norm, w_in, q_latent_norm, w_q_up, kv_latent_norm, w_kv_up,
               q_head_norm, k_head_norm, w_branch_mla, w_branch_sb, w_out, ffn2_norm, ffn2_w_in, ffn2_w_out,
               ple_norm, w_ple_gate, w_ple_proj)
    h = x.reshape(b * s, D_MODEL)
    for i in range(p.shape[0]):
        h = _layer(h, p[i].reshape(b * s, -1), rope, tile, sb_tile, b, s, *(w[i] for w in weights))
    return h.reshape(b, s, D_MODEL)
```

```python
import functools
import math

import jax
import jax.numpy as jnp
from jax import lax
from jax.experimental import pallas as pl
from jax.experimental.pallas import tpu as pltpu

D_MODEL = 1024
D_FF = 2816
FFN_RES_WEIGHT = 0.5
NORM_EPS = 1e-6

MLA_HEADS = 8
MLA_NOPE = 64
MLA_ROPE = 32
MLA_QK = MLA_NOPE + MLA_ROPE
MLA_V = 64
Q_LORA = 384
KV_LORA = 256
ROPE_BASE = 10000.0

SB_HEADS = 8
SB_HEAD_DIM = 64
SB_WIDTH = SB_HEADS * SB_HEAD_DIM
MLA_WIDTH = MLA_HEADS * MLA_V
V_HEAD = 64

LANES = 128
HEAD_PAD = LANES
MLA_QK_PAD = MLA_HEADS * HEAD_PAD
HALF_ROPE = MLA_ROPE // 2

VMEM_LIMIT_BYTES = 56 * 1024 * 1024
TOKEN_TILE = 512
MIX_TOKEN_TILE = 1024
MLA_TILE = 512
SB_TILE = 256
FF_CHUNKS = ((0, 1024), (1024, 1024), (2048, 768))
MASKED_SCORE = -1e30
SB_DEAD_LOG2_WEIGHT = 151.0

BF16 = jnp.bfloat16
F32 = jnp.float32


def _dot(a, b):
    return jnp.dot(a, b, preferred_element_type=F32)


def _rms(x, g, n=None):
    n = x.shape[-1] if n is None else n
    ms = jnp.sum(x * x, axis=-1, keepdims=True) * (1.0 / n)
    return x * lax.rsqrt(ms + NORM_EPS) * g


def _sigmoid(x):
    return 0.5 * jnp.tanh(0.5 * x) + 0.5


def _const_spec(shape):
    nd = len(shape)
    return pl.BlockSpec(shape, lambda *_: (0,) * nd, pipeline_mode=pl.Buffered(1))


def _row_spec(tile, width):
    return pl.BlockSpec((tile, width), lambda i: (i, 0))


def _params(n_axes):
    return pltpu.CompilerParams(dimension_semantics=("arbitrary",) * n_axes,
                                vmem_limit_bytes=VMEM_LIMIT_BYTES)


def _ffn_kernel(*refs, with_merge, with_ple):
    refs = list(refs)
    take = lambda n: [refs.pop(0) for _ in range(n)]
    (h_ref,) = take(1)
    h = h_ref[...]
    if with_merge:
        gmix_ref, wg_ref, om_ref, os_ref, wbm_ref, wbs_ref, wo_ref = take(7)
        gates = _sigmoid(_dot(_rms(h, gmix_ref[...]).astype(BF16), wg_ref[...]))
        merged = (gates[:, :D_MODEL] * _dot(om_ref[...], wbm_ref[...])
                  + gates[:, D_MODEL:] * _dot(os_ref[...], wbs_ref[...]))
        h = h + _dot(merged.astype(BF16), wo_ref[...])
    g_ref, win_ref, wout_ref = take(3)
    u = _rms(h, g_ref[...]).astype(BF16)
    acc = None
    for c0, cw in FF_CHUNKS:
        a = _dot(u, win_ref[:, c0:c0 + cw])
        b = _dot(u, win_ref[:, D_FF + c0:D_FF + c0 + cw])
        y = _dot((a * _sigmoid(a) * b).astype(BF16), wout_ref[c0:c0 + cw, :])
        acc = y if acc is None else acc + y
    h = h + FFN_RES_WEIGHT * acc
    if with_ple:
        p_ref, pg_ref, wpg_ref, wpp_ref = take(4)
        gate = _sigmoid(_dot(_rms(h, pg_ref[...]).astype(BF16), wpg_ref[...]))
        h = h + gate * _dot(p_ref[...].astype(BF16), wpp_ref[...])
    (o_ref,) = refs
    o_ref[...] = h


def _ffn(h, g, w_in, w_out, merge=None, ple=None):
    t = h.shape[0]
    tm = min(TOKEN_TILE, t)
    tiled = lambda a: (a, _row_spec(tm, a.shape[1]))
    const = lambda a: (a, _const_spec(a.shape))
    ops = [tiled(h)]
    if merge is not None:
        gmix, wg, om, osb, wbm, wbs, wo = merge
        ops += [const(gmix), const(wg), tiled(om), tiled(osb), const(wbm), const(wbs), const(wo)]
    ops += [const(g), const(w_in), const(w_out)]
    if ple is not None:
        p, pg, wpg, wpp = ple
        ops += [tiled(p), const(pg), const(wpg), const(wpp)]
    return pl.pallas_call(
        functools.partial(_ffn_kernel, with_merge=merge is not None, with_ple=ple is not None),
        out_shape=jax.ShapeDtypeStruct(h.shape, F32),
        grid=(t // tm,),
        in_specs=[spec for _, spec in ops],
        out_specs=_row_spec(tm, D_MODEL),
        compiler_params=_params(1),
        name="merge_ffn_ple" if merge is not None else "ffn",
    )(*[a for a, _ in ops])


def _head_norm_rope_t(xt, g_col, cos_t, sin_t):
    ms = jnp.sum(xt * xt, axis=0, keepdims=True) * (1.0 / MLA_QK)
    xt = xt * lax.rsqrt(ms + NORM_EPS) * g_col
    x1, x2 = xt[MLA_NOPE:MLA_NOPE + HALF_ROPE], xt[MLA_NOPE + HALF_ROPE:MLA_QK]
    return jnp.concatenate([xt[:MLA_NOPE], x1 * cos_t - x2 * sin_t, x2 * cos_t + x1 * sin_t, xt[MLA_QK:]], axis=0)


def _store_blocks_transposed(t_ref, v, tile, col0=0, finish=lambda c, vt: vt):
    for c in range(v.shape[0] // tile):
        t_ref[c, col0:col0 + v.shape[1], :] = finish(c, v[c * tile:(c + 1) * tile, :].T).astype(BF16)


def _rows_via_transposed(v, tile, finish):
    return jnp.concatenate([finish(c, v[c * tile:(c + 1) * tile, :].T).T for c in range(v.shape[0] // tile)], axis=0)


def _mix_proj_kernel(h_ref, gmix_ref, wlat_ref, wsb_ref, gq_ref, wq_ref, gkv_ref, wkv_ref, gqh_ref, gkh_ref,
                     cost_ref, sint_ref, qt_ref, k_ref, vt_ref, sqt_ref, sk_ref, svt_ref, *, tile, sb_tile):
    u = _rms(h_ref[...], gmix_ref[...]).astype(BF16)
    lat = _dot(u, wlat_ref[...])
    sb = _dot(u, wsb_ref[...])
    scale_sb = 1.0 / math.sqrt(SB_HEAD_DIM)
    _store_blocks_transposed(sqt_ref, sb[:, :SB_WIDTH] * scale_sb, sb_tile)
    sk_ref[...] = sb[:, SB_WIDTH:2 * SB_WIDTH].astype(BF16)
    _store_blocks_transposed(svt_ref, sb[:, 2 * SB_WIDTH:], sb_tile)

    c_q = lat[:, :Q_LORA]
    c_kv = lat[:, Q_LORA:Q_LORA + KV_LORA]
    k_rope = lat[:, Q_LORA + KV_LORA:]
    q = _dot(_rms(c_q, gq_ref[...]).astype(BF16), wq_ref[...])
    kv = _dot(_rms(c_kv, gkv_ref[...]).astype(BF16), wkv_ref[...])
    _store_blocks_transposed(vt_ref, kv[:, MLA_QK_PAD:], tile)
    gqh_col, gkh_col = gqh_ref[...], gkh_ref[...]
    q_finish = lambda c, xt: _head_norm_rope_t(xt, gqh_col, cost_ref[c], sint_ref[c])
    k_finish = lambda c, xt: _head_norm_rope_t(xt, gkh_col, cost_ref[c], sint_ref[c])
    for hd in range(MLA_HEADS):
        sl = slice(hd * HEAD_PAD, (hd + 1) * HEAD_PAD)
        _store_blocks_transposed(qt_ref, q[:, sl], tile, hd * HEAD_PAD, q_finish)
        k_ref[:, sl] = _rows_via_transposed(kv[:, sl] + k_rope, tile, k_finish).astype(BF16)


def _mix_proj(h, gmix, wlat, wsb, gq, wq, gkv, wkv, gqh_col, gkh_col, rope, tile, sb_tile):
    t = h.shape[0]
    tm = min(MIX_TOKEN_TILE, t)
    cos_t, sin_t = rope
    consts = [gmix, wlat, wsb, gq, wq, gkv, wkv, gqh_col, gkh_col]
    angle_blocks = pl.BlockSpec((tm // tile, HALF_ROPE, tile), lambda i: (i, 0, 0))
    rows = lambda w: (jax.ShapeDtypeStruct((t, w), BF16), _row_spec(tm, w))
    blocks_t = lambda w, tl: (jax.ShapeDtypeStruct((t // tl, w, tl), BF16),
                              pl.BlockSpec((tm // tl, w, tl), lambda i: (i, 0, 0)))
    outs = [blocks_t(MLA_QK_PAD, tile), rows(MLA_QK_PAD), blocks_t(MLA_WIDTH, tile),
            blocks_t(SB_WIDTH, sb_tile), rows(SB_WIDTH), blocks_t(SB_WIDTH, sb_tile)]
    return pl.pallas_call(
        functools.partial(_mix_proj_kernel, tile=tile, sb_tile=sb_tile),
        out_shape=[o[0] for o in outs],
        grid=(t // tm,),
        in_specs=[_row_spec(tm, D_MODEL)] + [_const_spec(c.shape) for c in consts]
                 + [angle_blocks, angle_blocks],
        out_specs=[o[1] for o in outs],
        compiler_params=_params(1),
        name="mix_proj",
    )(h, *consts, cos_t, sin_t)


def _tile_iotas(tile):
    key = lax.broadcasted_iota(jnp.int32, (tile, tile), 0)
    qry = lax.broadcasted_iota(jnp.int32, (tile, tile), 1)
    return key, qry


def _key_rows(j, tile):
    return pl.ds(pl.multiple_of(j * tile, tile), tile)


def _mla_attn_kernel(qt_ref, k_ref, vt_ref, o_ref, s_scr, p_scr, bias_scr, *, tile, n_tiles):
    c = math.log2(math.e) / math.sqrt(MLA_QK)
    heads = [slice(hd * HEAD_PAD, (hd + 1) * HEAD_PAD) for hd in range(2)]
    n_steps = n_tiles * (n_tiles + 1) // 2

    def scores(i, j, slot, may_be_diagonal):
        k_rows = _key_rows(j, tile)
        tops = []
        for hd in range(2):
            s = _dot(k_ref[0, k_rows, heads[hd]], qt_ref[0, i, heads[hd], :])
            if may_be_diagonal:
                s = s + bias_scr[(i == j).astype(jnp.int32)]
            s_scr[slot, hd] = s
            tops.append(jnp.max(s, axis=0, keepdims=True))
        return tuple(tops)

    def weighted_values(j, slot, alphas, accs):
        return tuple(alphas[hd] * accs[hd] + _dot(vt_ref[0, j, hd * V_HEAD:(hd + 1) * V_HEAD, :], p_scr[slot, hd])
                     for hd in range(2))

    def step(slot, carry, may_be_diagonal=True):
        i, j, tops, ms, ls, alphas, accs = carry
        on_diag = j == i
        next_i, next_j = jnp.where(on_diag, i + 1, i), jnp.where(on_diag, 0, j + 1)
        next_tops = scores(jnp.minimum(next_i, n_tiles - 1), next_j, 1 - slot, may_be_diagonal)
        prev_j = jnp.maximum(jnp.where(j > 0, j - 1, i - 1), 0)
        accs = weighted_values(prev_j, 1 - slot, alphas, accs)
        new_tile = jnp.logical_and(j == 0, i > 0)
        done = (new_tile, jnp.maximum(i - 1, 0), ls, accs)
        keep = jnp.where(new_tile, 0.0, 1.0)
        ms = tuple(m * keep + MASKED_SCORE * (1.0 - keep) for m in ms)
        out = []
        for hd in range(2):
            m_new = jnp.maximum(ms[hd], tops[hd])
            alpha = jnp.exp2((ms[hd] - m_new) * c)
            p = jnp.exp2((s_scr[slot, hd] - m_new) * c)
            p_scr[slot, hd] = p.astype(BF16)
            out.append((m_new, alpha * ls[hd] + jnp.sum(p, axis=0, keepdims=True), alpha))
        ms, ls, alphas = zip(*out)
        return (next_i, next_j, next_tops, ms, ls, alphas, accs), done

    def store_tile(i, ls, accs):
        o_t = jnp.concatenate([accs[hd] / ls[hd] for hd in range(2)], axis=0)
        o_ref[0, _key_rows(i, tile), :] = o_t.T.astype(BF16)

    def store_if_done(done):
        flag, i, ls, accs = done
        pl.when(flag)(lambda: store_tile(i, ls, accs))

    key, qry = _tile_iotas(tile)
    bias_scr[0] = jnp.zeros((tile, tile), F32)
    bias_scr[1] = jnp.where(key <= qry, 0.0, MASKED_SCORE)
    p_scr[1] = jnp.zeros(p_scr.shape[1:], BF16)
    zero = jnp.int32(0)
    tops = scores(zero, zero, 0, True)
    row = lambda v: (jnp.full((1, tile), v, F32),) * 2
    carry = (zero, zero, tops, row(MASKED_SCORE), row(0.0), row(1.0), (jnp.zeros((V_HEAD, tile), F32),) * 2)
    slots = (0, 1)
    if n_steps % 2:
        carry, done = step(0, carry)
        store_if_done(done)
        slots = (1, 0)

    def two_steps(carry, may_be_diagonal):
        carry, done_a = step(slots[0], carry, may_be_diagonal)
        carry, done_b = step(slots[1], carry, may_be_diagonal)
        store_if_done(done_a)
        store_if_done(done_b)
        return carry

    def trip(_, carry):
        i, j = carry[0], carry[1]
        gap = i - j
        diagonal_ahead = jnp.logical_or(jnp.logical_or(gap == 1, gap == 2), i == 0)
        return lax.cond(diagonal_ahead, functools.partial(two_steps, may_be_diagonal=True),
                        functools.partial(two_steps, may_be_diagonal=False), carry)

    _, _, _, _, ls, alphas, accs = lax.fori_loop(0, n_steps // 2, trip, carry)
    accs = weighted_values(n_tiles - 1, (n_steps - 1) % 2, alphas, accs)
    store_tile(n_tiles - 1, ls, accs)


def _sb_attn_kernel(qt_ref, k_ref, vt_ref, o_ref, *, tile, n_tiles):
    log2e = math.log2(math.e)
    key, qry = _tile_iotas(tile)
    strict = key < qry
    from_key = (qry >= key).astype(BF16)
    low = lax.broadcasted_iota(jnp.int32, (LANES, tile), 0) < V_HEAD

    def query_tile(i, _):
        q_pair = qt_ref[0, i]
        qs = [jnp.where(low, q_pair, jnp.zeros_like(q_pair)), jnp.where(low, jnp.zeros_like(q_pair), q_pair)]

        def two_blocks(j, deads, accs, masked):
            has_second = j >= 1
            idx = (j, jnp.maximum(j - 1, 0))
            k_blks = [k_ref[0, _key_rows(jb, tile), :] for jb in idx]
            z2s = [[_dot(k_blks[b], qs[hd]) * log2e for hd in range(2)] for b in range(2)]
            incl = [[None, None], [None, None]]
            for b in range(2):
                for hd in range(2):
                    z2 = z2s[b][hd]
                    x = jnp.maximum(z2, 0.0) + jnp.log2(1.0 + jnp.exp2(-jnp.abs(z2)))
                    if masked and b == 0:
                        x = jnp.where(strict, x, 0.0)
                    hi = x.astype(BF16)
                    lo = (x - hi.astype(F32)).astype(BF16)
                    incl[b][hd] = _dot(from_key, hi) + _dot(from_key, lo)
            new_deads, new_accs = [], []
            for hd in range(2):
                dead = [deads[hd], jnp.where(has_second, deads[hd] + incl[0][hd][0:1, :], jnp.inf)]
                acc = accs[hd]
                for b in range(2):
                    a = jnp.exp2(z2s[b][hd] - incl[b][hd] - dead[b])
                    if masked and b == 0:
                        a = jnp.where(strict, a, 0.0)
                    acc = acc + _dot(vt_ref[0, idx[b], hd * V_HEAD:(hd + 1) * V_HEAD, :], a.astype(BF16))
                new_deads.append(dead[1] + incl[1][hd][0:1, :])
                new_accs.append(acc)
            return tuple(new_deads), tuple(new_accs)

        def alive(deads):
            return (jnp.minimum(jnp.min(deads[0]), jnp.min(deads[1])) <= SB_DEAD_LOG2_WEIGHT).astype(jnp.int32)

        deads, accs = two_blocks(i, (jnp.zeros((1, tile), F32),) * 2, (jnp.zeros((V_HEAD, tile), F32),) * 2, True)

        def more(carry):
            j, live, _, _ = carry
            return jnp.logical_and(j >= 0, live > 0)

        def body(carry):
            j, _, deads, accs = carry
            deads, accs = two_blocks(j, deads, accs, False)
            return j - 2, alive(deads), deads, accs

        _, _, _, accs = lax.while_loop(more, body, (i - 2, alive(deads), deads, accs))
        o_t = jnp.concatenate(accs, axis=0)
        o_ref[0, _key_rows(i, tile), :] = o_t.T.astype(BF16)
        return 0

    lax.fori_loop(0, n_tiles, query_tile, 0)


def _blocks_t_spec(n_tiles, rows, tile):
    return pl.BlockSpec((1, n_tiles, rows, tile), lambda b, p: (b, 0, p, 0))


def _mla_attention(qt, k, vt):
    b, n_tiles, _, tile = qt.shape
    s = n_tiles * tile
    return pl.pallas_call(
        functools.partial(_mla_attn_kernel, tile=tile, n_tiles=n_tiles),
        out_shape=jax.ShapeDtypeStruct((b, s, MLA_WIDTH), BF16),
        grid=(b, MLA_HEADS // 2),
        in_specs=[_blocks_t_spec(n_tiles, 2 * HEAD_PAD, tile),
                  pl.BlockSpec((1, s, 2 * HEAD_PAD), lambda b, p: (b, 0, p)),
                  _blocks_t_spec(n_tiles, LANES, tile)],
        out_specs=pl.BlockSpec((1, s, LANES), lambda b, p: (b, 0, p)),
        scratch_shapes=[pltpu.VMEM((2, 2, tile, tile), F32), pltpu.VMEM((2, 2, tile, tile), BF16),
                        pltpu.VMEM((2, tile, tile), F32)],
        compiler_params=_params(2),
        name="mla_attn",
    )(qt, k, vt)


def _sb_attention(qt, k, vt):
    b, n_tiles, _, tile = qt.shape
    s = n_tiles * tile
    seq_pair = pl.BlockSpec((1, s, LANES), lambda b, p: (b, 0, p))
    return pl.pallas_call(
        functools.partial(_sb_attn_kernel, tile=tile, n_tiles=n_tiles),
        out_shape=jax.ShapeDtypeStruct((b, s, SB_WIDTH), BF16),
        grid=(b, SB_HEADS // 2),
        in_specs=[_blocks_t_spec(n_tiles, LANES, tile), seq_pair, _blocks_t_spec(n_tiles, LANES, tile)],
        out_specs=seq_pair,
        compiler_params=_params(2),
        name="sb_attn",
    )(qt, k, vt)


def _pad_heads(w, head_dim, real):
    lead = w.shape[:-1]
    w = w.reshape(lead + (-1, head_dim))[..., :real]
    w = jnp.pad(w, [(0, 0)] * len(lead) + [(0, 0), (0, HEAD_PAD - real)])
    return w.reshape(lead + (-1,))


def _rope_tables(positions, tile):
    inv_freq = ROPE_BASE ** (-jnp.arange(0, MLA_ROPE, 2, dtype=F32) / MLA_ROPE)
    ang_t = inv_freq.reshape(-1, 1) * positions.astype(F32).reshape(1, -1)
    blocks_t = lambda a: a.reshape(HALF_ROPE, -1, tile).transpose(1, 0, 2)
    return blocks_t(jnp.cos(ang_t)), blocks_t(jnp.sin(ang_t))


def _layer(h, p, rope, tile, sb_tile, b, s, ffn1_norm, ffn1_w_in, ffn1_w_out, mix_norm, w_in, q_latent_norm, w_q_up,
           kv_latent_norm, w_kv_up, q_head_norm, k_head_norm, w_branch_mla, w_branch_sb, w_out, ffn2_norm,
           ffn2_w_in, ffn2_w_out, ple_norm, w_ple_gate, w_ple_proj):
    row = lambda g: g.reshape(1, -1)
    c0, c1, c2, c3 = Q_LORA, Q_LORA + KV_LORA, Q_LORA + KV_LORA + MLA_ROPE, Q_LORA + KV_LORA + MLA_ROPE + 3 * SB_WIDTH
    w_krope = jnp.pad(w_in[:, c1:c2], ((0, 0), (MLA_NOPE, HEAD_PAD - MLA_QK)))
    w_lat = jnp.concatenate([w_in[:, :c1], w_krope], axis=1).astype(BF16)
    w_sb = w_in[:, c2:c3].astype(BF16)
    w_gates = w_in[:, c3:].astype(BF16)
    w_q = _pad_heads(w_q_up, MLA_QK, MLA_QK).astype(BF16)
    kv_heads = w_kv_up.reshape(KV_LORA, MLA_HEADS, MLA_NOPE + MLA_V)
    w_knope = _pad_heads(kv_heads[..., :MLA_NOPE].reshape(KV_LORA, -1), MLA_NOPE, MLA_NOPE)
    w_kv = jnp.concatenate([w_knope, kv_heads[..., MLA_NOPE:].reshape(KV_LORA, -1)], axis=1).astype(BF16)
    gqh_col = jnp.pad(q_head_norm, (0, HEAD_PAD - MLA_QK)).reshape(-1, 1)
    gkh_col = jnp.pad(k_head_norm, (0, HEAD_PAD - MLA_QK)).reshape(-1, 1)

    h = _ffn(h, row(ffn1_norm), ffn1_w_in.astype(BF16), ffn1_w_out.astype(BF16))
    qt, k, vt, sqt, sk, svt = _mix_proj(h, row(mix_norm), w_lat, w_sb, row(q_latent_norm), w_q, row(kv_latent_norm),
                                        w_kv, gqh_col, gkh_col, rope, tile, sb_tile)
    seq = lambda a: a.reshape(b, s, a.shape[-1])
    blocks = lambda a: a.reshape(b, -1, *a.shape[1:])
    o_mla = _mla_attention(blocks(qt), seq(k), blocks(vt))
    o_sb = _sb_attention(blocks(sqt), seq(sk), blocks(svt))
    return _ffn(h, row(ffn2_norm), ffn2_w_in.astype(BF16), ffn2_w_out.astype(BF16),
                merge=(row(mix_norm), w_gates, o_mla.reshape(b * s, -1), o_sb.reshape(b * s, -1),
                       w_branch_mla.astype(BF16), w_branch_sb.astype(BF16), w_out.astype(BF16)),
                ple=(p, row(ple_norm), w_ple_gate.astype(BF16), w_ple_proj.astype(BF16)))


def kernel(x, p, positions, ffn1_norm, ffn1_w_in, ffn1_w_out, mix_norm, w_in, q_latent_norm, w_q_up, kv_latent_norm, w_kv_up, q_head_norm, k_head_norm, w_branch_mla, w_branch_sb, w_out, ffn2_norm, ffn2_w_in, ffn2_w_out, ple_norm, w_ple_gate, w_ple_proj):
    b, s, _ = x.shape
    tile, sb_tile = min(MLA_TILE, s), min(SB_TILE, s)
    rope = _rope_tables(positions, tile)
    weights = (ffn1_norm, ffn1_w_in, ffn1_w_out, mix_norm, w_in, q_latent_norm, w_q_up, kv_latent_norm, w_kv_up,
               q_head_norm, k_head_norm, w_branch_mla, w_branch_sb, w_out, ffn2_norm, ffn2_w_in, ffn2_w_out,
               ple_norm, w_ple_gate, w_ple_proj)
    h = x.reshape(b * s, D_MODEL)
    for i in range(p.shape[0]):
        h = _layer(h, p[i].reshape(b * s, -1), rope, tile, sb_tile, b, s, *(w[i] for w in weights))
    return h.reshape(b, s, D_MODEL)
```
